```python
import math
import jax, jax.numpy as jnp
from jax import lax
import numpy as np

D_MODEL = 1024
BATCH = 16
SEQ = 2048
DEPTH = 1

CTX_LEN = 256
GRID_W = 64

DIFF_HEADS = 8
DIFF_HD = 64
DIFF_QK_W = DIFF_HEADS * 2 * DIFF_HD
DIFF_V_W = DIFF_HEADS * 2 * DIFF_HD
NA_HEADS = 8
NA_HD = 64
NA_W = NA_HEADS * NA_HD
NA_WIN_H = 8
NA_WIN_W = 16
GATE_W = 2 * D_MODEL
Q_COLS = DIFF_QK_W + NA_W
KV_COLS = DIFF_QK_W + DIFF_V_W + 2 * NA_W
IN_W = Q_COLS + GATE_W + KV_COLS
FFN_HIDDEN = ((8 * D_MODEL + 3 * 256 - 1) // (3 * 256)) * 256
N_MOD = 6
Q_BLOCK = 128
ROPE_THETA = 10000.0
LN_EPS = 1e-5
ALPHA = (2.0 * DEPTH) ** 0.25
BETA = (8.0 * DEPTH) ** -0.25

kernel_name = "hybrid_diffattn_natten_dit_layer"


def layer_norm(x, g, b):
    xf = x.astype(jnp.float32)
    mu = jnp.mean(xf, -1, keepdims=True)
    var = jnp.mean(jnp.square(xf - mu), -1, keepdims=True)
    return ((xf - mu) * lax.rsqrt(var + LN_EPS)).astype(x.dtype) * g + b


def rms_norm(x, g):
    xf = x.astype(jnp.float32)
    y = xf * lax.rsqrt(jnp.mean(jnp.square(xf), -1, keepdims=True) + LN_EPS)
    return y.astype(x.dtype) * g


def modulate(x, shift, scale):
    return x * (1 + scale) + shift


def axial_rope_tables(n_tokens, dtype):
    t = jnp.arange(n_tokens)
    row = (t // GRID_W).astype(jnp.float32)
    col = (t % GRID_W).astype(jnp.float32)
    n_freq = DIFF_HD // 4
    inv = ROPE_THETA ** (-jnp.arange(n_freq, dtype=jnp.float32) / n_freq)
    ang = jnp.concatenate([row[:, None] * inv, col[:, None] * inv], axis=-1)
    return jnp.cos(ang).astype(dtype), jnp.sin(ang).astype(dtype)


def apply_axial_rope(x, cos, sin):
    B, L, H, d = x.shape
    xr = x.reshape(B, L, H, 2, 2, d // 4)
    x1, x2 = xr[..., 0, :], xr[..., 1, :]
    c = cos.reshape(L, 1, 2, d // 4)
    s = sin.reshape(L, 1, 2, d // 4)
    out = jnp.stack([x1 * c - x2 * s, x2 * c + x1 * s], axis=-2)
    return out.reshape(B, L, H, d)


def diff_attention(q, k, v, lam, lam_init, subln_g):
    B, Lq, H2, d = q.shape
    H = H2 // 2
    qb_size = min(Q_BLOCK, Lq)
    nb = Lq // qb_size
    scale = d ** -0.5
    qb = jnp.moveaxis(q.reshape(B, nb, qb_size, H2, d), 1, 0)

    def one_block(qi):
        s = jnp.einsum('bqhd,bkhd->bhqk', qi, k).astype(jnp.float32) * scale
        p = jax.nn.softmax(s, axis=-1).reshape(B, H, 2, qb_size, -1)
        a = (p[:, :, 0] - lam * p[:, :, 1]).astype(v.dtype)
        return jnp.einsum('bhqk,bkhe->bqhe', a, v)

    o = jnp.moveaxis(lax.map(one_block, qb), 0, 1).reshape(B, Lq, H, 2 * d)
    o = rms_norm(o, subln_g) * (1.0 - lam_init)
    return o.reshape(B, Lq, H * 2 * d)


def neighbourhood_attention(q, k, v, k_ctx, v_ctx, rpb):
    B, L, H, d = q.shape
    rows = L // GRID_W
    wh = min(NA_WIN_H, rows)
    scale = d ** -0.5
    qg = q.reshape(B, rows, GRID_W, H, d)
    kg = k.reshape(B, rows, GRID_W, H, d)
    vg = v.reshape(B, rows, GRID_W, H, d)
    col = jnp.arange(GRID_W)
    col_start = jnp.clip(col - NA_WIN_W // 2, 0, GRID_W - NA_WIN_W)
    col_ok = (col[None, :] >= col_start[:, None]) & (col[None, :] < col_start[:, None] + NA_WIN_W)
    col_idx = jnp.clip(col[None, :] - col[:, None] + NA_WIN_W - 1, 0, 2 * NA_WIN_W - 2)
    n_loc = wh * GRID_W

    def row_block(r):
        rs = jnp.clip(r - wh // 2, 0, rows - wh)
        kr = lax.dynamic_slice_in_dim(kg, rs, wh, axis=1)
        vr = lax.dynamic_slice_in_dim(vg, rs, wh, axis=1)
        qr = lax.dynamic_index_in_dim(qg, r, axis=1, keepdims=False)
        s_loc = jnp.einsum('bqhd,bjkhd->bhqjk', qr, kr).astype(jnp.float32) * scale
        row_off = rs + jnp.arange(wh) - r
        bias = rpb[:, row_off + NA_WIN_H - 1][:, :, col_idx]
        s_loc = s_loc + jnp.transpose(bias, (0, 2, 1, 3)).astype(jnp.float32)[None]
        s_loc = jnp.where(col_ok[None, None, :, None, :], s_loc, -jnp.inf)
        s_ctx = jnp.einsum('bqhd,bkhd->bhqk', qr, k_ctx).astype(jnp.float32) * scale
        s = jnp.concatenate([s_loc.reshape(B, H, GRID_W, n_loc), s_ctx], axis=-1)
        p = jax.nn.softmax(s, axis=-1).astype(v.dtype)
        p_loc = p[..., :n_loc].reshape(B, H, GRID_W, wh, GRID_W)
        p_ctx = p[..., n_loc:]
        return (jnp.einsum('bhqjk,bjkhd->bqhd', p_loc, vr)
                + jnp.einsum('bhqk,bkhd->bqhd', p_ctx, v_ctx))

    o = lax.map(row_block, jnp.arange(rows))
    return jnp.moveaxis(o, 0, 1).reshape(B, L, H * d)


def context_attention(q, k, v):
    B, Lq, H, d = q.shape
    s = jnp.einsum('bqhd,bkhd->bhqk', q, k).astype(jnp.float32) * d ** -0.5
    p = jax.nn.softmax(s, axis=-1).astype(v.dtype)
    return jnp.einsum('bhqk,bkhd->bqhd', p, v).reshape(B, Lq, H * d)


def split_queries_gates(p):
    B, L = p.shape[:2]
    q_d = p[..., :DIFF_QK_W].reshape(B, L, 2 * DIFF_HEADS, DIFF_HD)
    q_n = p[..., DIFF_QK_W:Q_COLS].reshape(B, L, NA_HEADS, NA_HD)
    gate_logits = p[..., Q_COLS:Q_COLS + GATE_W]
    return q_d, q_n, gate_logits


def split_kv(p):
    B, L = p.shape[:2]
    k_d = p[..., :DIFF_QK_W].reshape(B, L, 2 * DIFF_HEADS, DIFF_HD)
    v_d = p[..., DIFF_QK_W:DIFF_QK_W + DIFF_V_W].reshape(B, L, DIFF_HEADS, 2 * DIFF_HD)
    o = DIFF_QK_W + DIFF_V_W
    k_n = p[..., o:o + NA_W].reshape(B, L, NA_HEADS, NA_HD)
    v_n = p[..., o + NA_W:].reshape(B, L, NA_HEADS, NA_HD)
    return k_d, v_d, k_n, v_n


def merge_branches(o_d, o_n, gate_logits, b_gate, w_branch_diff, w_branch_na, w_out):
    g = jax.nn.sigmoid(gate_logits + b_gate)
    y = g[..., :D_MODEL] * (o_d @ w_branch_diff) + g[..., D_MODEL:] * (o_n @ w_branch_na)
    return y @ w_out


def swiglu(h, w_ffn_in, w_ffn_out):
    gu = h @ w_ffn_in
    return (jax.nn.silu(gu[..., :FFN_HIDDEN]) * gu[..., FFN_HIDDEN:]) @ w_ffn_out


def hybrid_layer(x, ctx, c_silu, c_ctx_silu, w_mod, b_mod, w_in, b_gate, lam_q1, lam_k1, lam_q2,
                 lam_k2, subln_g, na_rpb, w_branch_diff, w_branch_na, w_out, ln1_g, ln1_b,
                 w_ffn_in, w_ffn_out, ln2_g, ln2_b, layer_idx, update_ctx):
    B, L, D = x.shape
    mod_x = (c_silu @ w_mod + b_mod).reshape(B, 1, N_MOD, D)
    mod_c = (c_ctx_silu @ w_mod + b_mod).reshape(1, 1, N_MOD, D)
    sh1, sc1, g1, sh2, sc2, g2 = [mod_x[:, :, i] for i in range(N_MOD)]
    csh1, csc1, cg1, csh2, csc2, cg2 = [mod_c[:, :, i] for i in range(N_MOD)]

    lam_init = 0.8 - 0.6 * math.exp(-0.3 * layer_idx)
    lam = (jnp.exp(jnp.sum(lam_q1.astype(jnp.float32) * lam_k1.astype(jnp.float32)))
           - jnp.exp(jnp.sum(lam_q2.astype(jnp.float32) * lam_k2.astype(jnp.float32))) + lam_init)

    hx = modulate(x, sh1, sc1)
    hc = modulate(ctx, csh1, csc1)
    px = hx @ w_in
    q_d, q_n, gl_x = split_queries_gates(px)
    k_d, v_d, k_n, v_n = split_kv(px[..., Q_COLS + GATE_W:])
    kc_d, vc_d, kc_n, vc_n = split_kv(hc @ w_in[:, Q_COLS + GATE_W:])
    cos, sin = axial_rope_tables(L, x.dtype)
    q_d = apply_axial_rope(q_d, cos, sin)
    k_d = apply_axial_rope(k_d, cos, sin)
    o_d = diff_attention(q_d, jnp.concatenate([k_d, kc_d], axis=1),
                         jnp.concatenate([v_d, vc_d], axis=1), lam, lam_init, subln_g)
    o_n = neighbourhood_attention(q_n, k_n, v_n, kc_n, vc_n, na_rpb)
    mix_x = merge_branches(o_d, o_n, gl_x, b_gate, w_branch_diff, w_branch_na, w_out)
    x_mid = layer_norm(ALPHA * x + g1 * mix_x, ln1_g, ln1_b)

    ffn_x = swiglu(modulate(x_mid, sh2, sc2), w_ffn_in, w_ffn_out)
    x_out = layer_norm(ALPHA * x_mid + g2 * ffn_x, ln2_g, ln2_b)

    if update_ctx:
        qc_d, qc_n, gl_c = split_queries_gates(hc @ w_in[:, :Q_COLS + GATE_W])
        oc_d = diff_attention(qc_d, kc_d, vc_d, lam, lam_init, subln_g)
        oc_n = context_attention(qc_n, kc_n, vc_n)
        mix_c = merge_branches(oc_d, oc_n, gl_c, b_gate, w_branch_diff, w_branch_na, w_out)
        c_mid = layer_norm(ALPHA * ctx + cg1 * mix_c, ln1_g, ln1_b)
        ffn_c = swiglu(modulate(c_mid, csh2, csc2), w_ffn_in, w_ffn_out)
        ctx = layer_norm(ALPHA * c_mid + cg2 * ffn_c, ln2_g, ln2_b)
    return x_out, ctx


def setup_inputs(seed: int = 0) -> dict:
    key = jax.random.key(seed)
    ks = jax.random.split(key, 24)
    f32 = jnp.float32
    nrm = lambda k, shape: jax.random.normal(k, shape, dtype=f32)
    D = D_MODEL
    v_scale = jnp.concatenate([
        jnp.ones((Q_COLS + GATE_W + DIFF_QK_W,), f32), jnp.full((DIFF_V_W,), BETA, f32),
        jnp.ones((NA_W,), f32), jnp.full((NA_W,), BETA, f32)])
    return {
        "x": nrm(ks[0], (BATCH, SEQ, D)),
        "c": nrm(ks[1], (BATCH, D)),
        "ctx": nrm(ks[2], (BATCH, CTX_LEN, D)),
        "c_ctx": nrm(ks[3], (D,)),
        "w_mod": nrm(ks[4], (DEPTH, D, N_MOD * D)) * (0.2 * D ** -0.5),
        "b_mod": nrm(ks[5], (DEPTH, N_MOD * D)) * 0.01,
        "w_in": nrm(ks[6], (DEPTH, D, IN_W)) * D ** -0.5 * v_scale,
        "b_gate": nrm(ks[7], (DEPTH, GATE_W)) * 0.01,
        "lam_q1": nrm(ks[8], (DEPTH, DIFF_HD)) * 0.1,
        "lam_k1": nrm(ks[9], (DEPTH, DIFF_HD)) * 0.1,
        "lam_q2": nrm(ks[10], (DEPTH, DIFF_HD)) * 0.1,
        "lam_k2": nrm(ks[11], (DEPTH, DIFF_HD)) * 0.1,
        "subln_g": 1.0 + 0.02 * nrm(ks[12], (DEPTH, 2 * DIFF_HD)),
        "na_rpb": nrm(ks[13], (DEPTH, NA_HEADS, 2 * NA_WIN_H - 1, 2 * NA_WIN_W - 1)) * 0.02,
        "w_branch_diff": nrm(ks[14], (DEPTH, DIFF_V_W, D)) * DIFF_V_W ** -0.5 * BETA,
        "w_branch_na": nrm(ks[15], (DEPTH, NA_W, D)) * NA_W ** -0.5 * BETA,
        "w_out": nrm(ks[16], (DEPTH, D, D)) * D ** -0.5 * BETA,
        "ln1_g": 1.0 + 0.02 * nrm(ks[17], (DEPTH, D)),
        "ln1_b": 0.02 * nrm(ks[18], (DEPTH, D)),
        "w_ffn_in": nrm(ks[19], (DEPTH, D, 2 * FFN_HIDDEN)) * D ** -0.5 * BETA,
        "w_ffn_out": nrm(ks[20], (DEPTH, FFN_HIDDEN, D)) * FFN_HIDDEN ** -0.5 * BETA,
        "ln2_g": 1.0 + 0.02 * nrm(ks[21], (DEPTH, D)),
        "ln2_b": 0.02 * nrm(ks[22], (DEPTH, D)),
    }


def reference(x, c, ctx, c_ctx, w_mod, b_mod, w_in, b_gate, lam_q1, lam_k1, lam_q2, lam_k2,
              subln_g, na_rpb, w_branch_diff, w_branch_na, w_out, ln1_g, ln1_b, w_ffn_in,
              w_ffn_out, ln2_g, ln2_b):
    c_silu = jax.nn.silu(c)
    c_ctx_silu = jax.nn.silu(c_ctx)
    for l in range(DEPTH):
        x, ctx = hybrid_layer(
            x, ctx, c_silu, c_ctx_silu, w_mod[l], b_mod[l], w_in[l], b_gate[l], lam_q1[l],
            lam_k1[l], lam_q2[l], lam_k2[l], subln_g[l], na_rpb[l], w_branch_diff[l],
            w_branch_na[l], w_out[l], ln1_g[l], ln1_b[l], w_ffn_in[l], w_ffn_out[l], ln2_g[l],
            ln2_b[l], layer_idx=l, update_ctx=(l < DEPTH - 1))
    return x
```

```python
import functools
import math

import numpy as np
import jax
import jax.numpy as jnp
from jax import lax
from jax.experimental import pallas as pl
from jax.experimental.pallas import tpu as pltpu

D_MODEL = 1024
SEQ = 2048
CTX_LEN = 256
GRID_W = 64
ROWS = SEQ // GRID_W
DEPTH = 1

DIFF_HEADS = 8
DIFF_HD = 64
DIFF_QK_W = DIFF_HEADS * 2 * DIFF_HD
DIFF_V_W = DIFF_HEADS * 2 * DIFF_HD
NA_HEADS = 8
NA_HD = 64
NA_W = NA_HEADS * NA_HD
NA_WIN_H = 8
NA_WIN_W = 16
GATE_W = 2 * D_MODEL
Q_COLS = DIFF_QK_W + NA_W
KV_COLS = DIFF_QK_W + DIFF_V_W + 2 * NA_W
IN_W = Q_COLS + GATE_W + KV_COLS
FFN_HIDDEN = 2816
N_MOD = 6
ROPE_THETA = 10000.0
LN_EPS = 1e-5
ALPHA = (2.0 * DEPTH) ** 0.25

LANES = 128
NEG_BIG = -1e30
VMEM_LIMIT = 56 * 1024 * 1024

QD_BLK = 0
GD_BLK = DIFF_QK_W // LANES
GN_BLK = GD_BLK + D_MODEL // LANES
QN_BLK = GN_BLK + D_MODEL // LANES
KD_BLK = (Q_COLS + GATE_W) // LANES
VD_BLK = KD_BLK + DIFF_QK_W // LANES
KN_BLK = VD_BLK + DIFF_V_W // LANES
VN_BLK = KN_BLK + NA_W // LANES
C_KD_BLK = 0
C_VD_BLK = DIFF_QK_W // LANES
C_KN_BLK = C_VD_BLK + DIFF_V_W // LANES
C_VN_BLK = C_KN_BLK + NA_W // LANES

PLAIN, ROPE_SCALED, SCALED, GATE, ROPE = range(5)
PROJ_TN = 512
PROJ_TM = 1024
QK_SCALE = DIFF_HD ** -0.5


def _sigmoid(x):
    return 1.0 / (1.0 + jnp.exp(-x))


def _params(*sem):
    return pltpu.CompilerParams(dimension_semantics=sem, vmem_limit_bytes=VMEM_LIMIT)


def _mod_kernel(c_ref, w_ref, b_ref, o_ref):
    c = c_ref[...]
    cs = c * _sigmoid(c)
    o_ref[...] = jnp.dot(cs, w_ref[...], preferred_element_type=jnp.float32) + b_ref[...]


def _modulation(cc, w_mod, b_mod):
    n_rows = cc.shape[0]
    n = w_mod.shape[1]
    tn = 1536
    return pl.pallas_call(
        _mod_kernel,
        out_shape=jax.ShapeDtypeStruct((n_rows, n), jnp.float32),
        grid=(n // tn,),
        in_specs=[
            pl.BlockSpec((n_rows, D_MODEL), lambda j: (0, 0)),
            pl.BlockSpec((D_MODEL, tn), lambda j: (0, j)),
            pl.BlockSpec((1, tn), lambda j: (0, j)),
        ],
        out_specs=pl.BlockSpec((n_rows, tn), lambda j: (0, j)),
        compiler_params=_params("parallel"),
        name="modulation",
    )(cc, w_mod, b_mod.reshape(1, n))


def _rope_tables():
    t = jnp.arange(SEQ)
    row = (t // GRID_W).astype(jnp.float32)
    col = (t % GRID_W).astype(jnp.float32)
    n_freq = DIFF_HD // 4
    inv = ROPE_THETA ** (-jnp.arange(n_freq, dtype=jnp.float32) / n_freq)
    ang_r = row[:, None] * inv
    ang_c = col[:, None] * inv
    ang = jnp.concatenate([ang_r, ang_r, ang_c, ang_c], axis=-1)
    sign = jnp.asarray(np.tile(np.repeat([-1.0, 1.0], n_freq), 2), jnp.float32)
    cos = jnp.tile(jnp.cos(ang), (1, LANES // DIFF_HD))
    sin = jnp.tile(jnp.sin(ang) * sign, (1, LANES // DIFF_HD))
    return cos, sin


def _rope_chunk(a, cos, sin_signed, first_half):
    fwd = pltpu.roll(a, LANES - DIFF_HD // 4, 1)
    bwd = pltpu.roll(a, DIFF_HD // 4, 1)
    return a * cos + jnp.where(first_half, fwd, bwd) * sin_signed


def _proj_kernel(kinds, x_ref, mod_ref, w_ref, bias_ref, cos_ref, sin_ref, o_ref, hx_ref):
    j = pl.program_id(2)

    @pl.when(j == 0)
    def _():
        shift = mod_ref[0, 0:1, :]
        scale = mod_ref[0, 1:2, :]
        hx_ref[...] = (x_ref[0] * (1.0 + scale) + shift).astype(jnp.bfloat16)

    acc = jnp.dot(hx_ref[...], w_ref[...], preferred_element_type=jnp.float32)
    n_chunks = acc.shape[1] // LANES

    def rope(scale):
        lane = lax.broadcasted_iota(jnp.int32, (1, LANES), 1)
        first_half = (lane & (DIFF_HD // 4)) == 0
        cos = cos_ref[...]
        sin = sin_ref[...]
        for c in range(n_chunks):
            r = _rope_chunk(acc[:, c * LANES:(c + 1) * LANES], cos, sin, first_half)
            o_ref[0, :, c * LANES:(c + 1) * LANES] = (r * scale).astype(o_ref.dtype)

    for kind in sorted(set(kinds)):
        js = [jj for jj, k in enumerate(kinds) if k == kind]
        lo, hi = js[0], js[-1] + 1
        assert js == list(range(lo, hi))

        @pl.when((j >= lo) & (j < hi))
        def _(kind=kind):
            if kind == PLAIN:
                o_ref[0] = acc.astype(o_ref.dtype)
            elif kind == SCALED:
                o_ref[0] = (acc * QK_SCALE).astype(o_ref.dtype)
            elif kind == GATE:
                o_ref[0] = _sigmoid(acc + bias_ref[...]).astype(o_ref.dtype)
            elif kind == ROPE_SCALED:
                rope(QK_SCALE)
            elif kind == ROPE:
                rope(1.0)


def _projection(x, mod, w, bias, cos, sin, kinds):
    B, L, D = x.shape
    n = w.shape[1]
    tm = min(PROJ_TM, L)
    tn = PROJ_TN
    assert n == tn * len(kinds) and L % tm == 0
    per_batch = mod.shape[0] > 1
    return pl.pallas_call(
        functools.partial(_proj_kernel, kinds),
        out_shape=jax.ShapeDtypeStruct((B, L, n), jnp.bfloat16),
        grid=(B, L // tm, n // tn),
        in_specs=[
            pl.BlockSpec((1, tm, D), lambda b, i, j: (b, i, 0)),
            pl.BlockSpec((1, N_MOD, D), (lambda b, i, j: (b, 0, 0)) if per_batch else (lambda b, i, j: (0, 0, 0))),
            pl.BlockSpec((D, tn), lambda b, i, j: (0, j)),
            pl.BlockSpec((1, tn), lambda b, i, j: (0, j)),
            pl.BlockSpec((tm, LANES), lambda b, i, j: (i, 0)),
            pl.BlockSpec((tm, LANES), lambda b, i, j: (i, 0)),
        ],
        out_specs=pl.BlockSpec((1, tm, tn), lambda b, i, j: (b, i, j)),
        scratch_shapes=[pltpu.VMEM((tm, D), jnp.bfloat16)],
        compiler_params=_params("parallel", "parallel", "arbitrary"),
        name="projection",
    )(x, mod, w, bias, cos, sin)


DIFF_TQ = 256
DIFF_TK = 512


def _nt_dot(a, b):
    return lax.dot_general(a, b, (((1,), (1,)), ((), ())), preferred_element_type=jnp.float32)


def _diff_attn_kernel(lam_init, lq1_ref, lk1_ref, lq2_ref, lk2_ref, g_ref,
                      q_ref, kx_ref, vx_ref, kc_ref, vc_ref, o_ref):
    q = q_ref[0]
    lane = lax.broadcasted_iota(jnp.int32, (1, LANES), 1)
    zero = jnp.zeros_like(q)
    q_parts = (jnp.where(lane < DIFF_HD, q, zero), jnp.where(lane >= DIFF_HD, q, zero))

    tiles = [(kx_ref, vx_ref, t * DIFF_TK, DIFF_TK) for t in range(SEQ // DIFF_TK)]
    tiles.append((kc_ref, vc_ref, 0, CTX_LEN))

    outs = []
    for qp in q_parts:
        m = l = acc = None
        for k_ref, v_ref, off, size in tiles:
            k = k_ref[0, off:off + size, :]
            v = v_ref[0, off:off + size, :]
            s = _nt_dot(qp, k)
            m_tile = jnp.max(s, axis=-1, keepdims=True)
            if m is None:
                m = m_tile
                e = jnp.exp(s - m)
                l = jnp.sum(e, axis=-1, keepdims=True)
                acc = jnp.dot(e.astype(jnp.bfloat16), v, preferred_element_type=jnp.float32)
            else:
                m_new = jnp.maximum(m, m_tile)
                a = jnp.exp(m - m_new)
                e = jnp.exp(s - m_new)
                l = a * l + jnp.sum(e, axis=-1, keepdims=True)
                acc = a * acc + jnp.dot(e.astype(jnp.bfloat16), v, preferred_element_type=jnp.float32)
                m = m_new
        outs.append(acc * (1.0 / l))

    lam = (jnp.exp(jnp.sum(lq1_ref[...] * lk1_ref[...], axis=-1, keepdims=True))
           - jnp.exp(jnp.sum(lq2_ref[...] * lk2_ref[...], axis=-1, keepdims=True)) + lam_init)
    o = outs[0] - lam * outs[1]
    y = o * lax.rsqrt(jnp.mean(o * o, axis=-1, keepdims=True) + LN_EPS)
    o_ref[0] = (y * g_ref[...] * (1.0 - lam_init)).astype(o_ref.dtype)


def _diff_attention(px, pc, lam_q1, lam_k1, lam_q2, lam_k2, subln_g, lam_init):
    B = px.shape[0]
    tq = DIFF_TQ
    vec = lambda a: a.reshape(1, -1)
    small = lambda n: pl.BlockSpec((1, n), lambda b, h, i: (0, 0))
    return pl.pallas_call(
        functools.partial(_diff_attn_kernel, lam_init),
        out_shape=jax.ShapeDtypeStruct((B, SEQ, DIFF_V_W), jnp.bfloat16),
        grid=(B, DIFF_HEADS, SEQ // tq),
        in_specs=[
            small(DIFF_HD), small(DIFF_HD), small(DIFF_HD), small(DIFF_HD), small(2 * DIFF_HD),
            pl.BlockSpec((1, tq, LANES), lambda b, h, i: (b, i, QD_BLK + h)),
            pl.BlockSpec((1, SEQ, LANES), lambda b, h, i: (b, 0, KD_BLK + h)),
            pl.BlockSpec((1, SEQ, LANES), lambda b, h, i: (b, 0, VD_BLK + h)),
            pl.BlockSpec((1, CTX_LEN, LANES), lambda b, h, i: (b, 0, C_KD_BLK + h)),
            pl.BlockSpec((1, CTX_LEN, LANES), lambda b, h, i: (b, 0, C_VD_BLK + h)),
        ],
        out_specs=pl.BlockSpec((1, tq, LANES), lambda b, h, i: (b, i, h)),
        compiler_params=_params("parallel", "parallel", "arbitrary"),
        name="diff_attention",
    )(vec(lam_q1), vec(lam_k1), vec(lam_q2), vec(lam_k2), vec(subln_g), px, px, px, pc, pc)


NA_QROWS = 8
NA_KROWS = 16
NA_TQ = NA_QROWS * GRID_W
NA_TK = NA_KROWS * GRID_W
NA_BLOCKS = ROWS // NA_QROWS


def _na_key_row_start(i):
    if isinstance(i, int):
        return min(max(i * NA_QROWS - NA_WIN_H // 2, 0), ROWS - NA_KROWS)
    return jnp.clip(i * NA_QROWS - NA_WIN_H // 2, 0, ROWS - NA_KROWS)


def _na_bias_tables(rpb):
    wh = min(NA_WIN_H, ROWS)
    col = np.arange(GRID_W)
    col_start = np.clip(col - NA_WIN_W // 2, 0, GRID_W - NA_WIN_W)
    col_ok = (col[None, :] >= col_start[:, None]) & (col[None, :] < col_start[:, None] + NA_WIN_W)
    col_idx = np.clip(col[None, :] - col[:, None] + NA_WIN_W - 1, 0, 2 * NA_WIN_W - 2)
    dr_idx = np.zeros((NA_BLOCKS, NA_QROWS, NA_KROWS), np.int32)
    row_ok = np.zeros((NA_BLOCKS, NA_QROWS, NA_KROWS), bool)
    for i in range(NA_BLOCKS):
        ks = _na_key_row_start(i)
        for a in range(NA_QROWS):
            qr = i * NA_QROWS + a
            rs = min(max(qr - wh // 2, 0), ROWS - wh)
            for c in range(NA_KROWS):
                kr = ks + c
                row_ok[i, a, c] = rs <= kr < rs + wh
                dr_idx[i, a, c] = min(max(kr - qr + NA_WIN_H - 1, 0), 2 * NA_WIN_H - 2)
    r = rpb[:, :, col_idx]
    t = r[:, dr_idx]
    ok = row_ok[:, :, :, None, None] & col_ok[None, None, None]
    t = jnp.where(ok[None], t, NEG_BIG)
    t = jnp.transpose(t, (0, 1, 2, 4, 3, 5))
    return t.reshape(NA_HEADS, NA_BLOCKS, NA_TQ, NA_TK).astype(jnp.float32)


def _na_kernel(bias_ref, q_ref, k_ref, v_ref, kc_ref, vc_ref, o_ref):
    i = pl.program_id(1)
    start = pl.multiple_of(_na_key_row_start(i) * GRID_W, 4 * GRID_W)
    q = q_ref[0]
    k = k_ref[0, pl.ds(start, NA_TK), :]
    v = v_ref[0, pl.ds(start, NA_TK), :]
    kc = kc_ref[0]
    vc = vc_ref[0]
    lane = lax.broadcasted_iota(jnp.int32, (1, LANES), 1)
    zero = jnp.zeros_like(q)
    outs = []
    for hh in range(2):
        in_head = (lane >= hh * NA_HD) & (lane < (hh + 1) * NA_HD)
        qh = jnp.where(in_head, q, zero)
        s_loc = _nt_dot(qh, k) + bias_ref[hh, 0]
        s_ctx = _nt_dot(qh, kc)
        m = jnp.maximum(jnp.max(s_loc, axis=-1, keepdims=True), jnp.max(s_ctx, axis=-1, keepdims=True))
        e_loc = jnp.exp(s_loc - m)
        e_ctx = jnp.exp(s_ctx - m)
        l = jnp.sum(e_loc, axis=-1, keepdims=True) + jnp.sum(e_ctx, axis=-1, keepdims=True)
        o = (jnp.dot(e_loc.astype(jnp.bfloat16), v, preferred_element_type=jnp.float32)
             + jnp.dot(e_ctx.astype(jnp.bfloat16), vc, preferred_element_type=jnp.float32))
        outs.append(o * (1.0 / l))
    o_ref[0] = jnp.where(lane < NA_HD, outs[0], outs[1]).astype(o_ref.dtype)


def _na_attention(px, pc, bias):
    B = px.shape[0]
    return pl.pallas_call(
        _na_kernel,
        out_shape=jax.ShapeDtypeStruct((B, SEQ, NA_W), jnp.bfloat16),
        grid=(NA_HEADS // 2, NA_BLOCKS, B),
        in_specs=[
            pl.BlockSpec((2, 1, NA_TQ, NA_TK), lambda hp, i, b: (hp, i, 0, 0)),
            pl.BlockSpec((1, NA_TQ, LANES), lambda hp, i, b: (b, i, QN_BLK + hp)),
            pl.BlockSpec((1, SEQ, LANES), lambda hp, i, b: (b, 0, KN_BLK + hp)),
            pl.BlockSpec((1, SEQ, LANES), lambda hp, i, b: (b, 0, VN_BLK + hp)),
            pl.BlockSpec((1, CTX_LEN, LANES), lambda hp, i, b: (b, 0, C_KN_BLK + hp)),
            pl.BlockSpec((1, CTX_LEN, LANES), lambda hp, i, b: (b, 0, C_VN_BLK + hp)),
        ],
        out_specs=pl.BlockSpec((1, NA_TQ, LANES), lambda hp, i, b: (b, i, hp)),
        compiler_params=_params("parallel", "parallel", "arbitrary"),
        name="na_attention",
    )(bias, px, px, px, pc, pc)


def _layer_norm(z, g, b):
    mu = jnp.mean(z, axis=-1, keepdims=True)
    zc = z - mu
    var = jnp.mean(zc * zc, axis=-1, keepdims=True)
    return zc * lax.rsqrt(var + LN_EPS) * g + b


MERGE_TM = 512


def _merge_kernel(x_ref, mod_ref, od_ref, on_ref, gd_ref, gn_ref, wbd_ref, wbn_ref, wo_ref,
                  lng_ref, lnb_ref, o_ref):
    yd = jnp.dot(od_ref[0], wbd_ref[...], preferred_element_type=jnp.float32)
    yn = jnp.dot(on_ref[0], wbn_ref[...], preferred_element_type=jnp.float32)
    y = gd_ref[0].astype(jnp.float32) * yd + gn_ref[0].astype(jnp.float32) * yn
    mix = jnp.dot(y.astype(jnp.bfloat16), wo_ref[...], preferred_element_type=jnp.float32)
    g1 = mod_ref[0, 2:3, :]
    o_ref[0] = _layer_norm(ALPHA * x_ref[0] + g1 * mix, lng_ref[...], lnb_ref[...])


def _merge(x, mod, o_d, o_n, px, wbd, wbn, wo, ln_g, ln_b):
    B, L, D = x.shape
    tm = MERGE_TM
    gblk = D // LANES
    const = lambda shape: pl.BlockSpec(shape, lambda b, i: (0,) * len(shape))
    return pl.pallas_call(
        _merge_kernel,
        out_shape=jax.ShapeDtypeStruct((B, L, D), jnp.float32),
        grid=(B, L // tm),
        in_specs=[
            pl.BlockSpec((1, tm, D), lambda b, i: (b, i, 0)),
            pl.BlockSpec((1, N_MOD, D), lambda b, i: (b, 0, 0)),
            pl.BlockSpec((1, tm, DIFF_V_W), lambda b, i: (b, i, 0)),
            pl.BlockSpec((1, tm, NA_W), lambda b, i: (b, i, 0)),
            pl.BlockSpec((1, tm, D), lambda b, i: (b, i, GD_BLK // gblk)),
            pl.BlockSpec((1, tm, D), lambda b, i: (b, i, GN_BLK // gblk)),
            const((DIFF_V_W, D)), const((NA_W, D)), const((D, D)), const((1, D)), const((1, D)),
        ],
        out_specs=pl.BlockSpec((1, tm, D), lambda b, i: (b, i, 0)),
        compiler_params=_params("parallel", "parallel"),
        name="merge",
    )(x, mod, o_d, o_n, px, px, wbd, wbn, wo, ln_g.reshape(1, D), ln_b.reshape(1, D))


FFN_TM = 512
FFN_CHUNK = FFN_HIDDEN // 2


def _ffn_kernel(x_ref, mod_ref, wg_ref, wu_ref, wo_ref, lng_ref, lnb_ref, o_ref):
    x = x_ref[0]
    shift = mod_ref[0, 3:4, :]
    scale = mod_ref[0, 4:5, :]
    g2 = mod_ref[0, 5:6, :]
    h = (x * (1.0 + scale) + shift).astype(jnp.bfloat16)
    out = None
    for c in range(FFN_HIDDEN // FFN_CHUNK):
        sl = slice(c * FFN_CHUNK, (c + 1) * FFN_CHUNK)
        gate = jnp.dot(h, wg_ref[:, sl], preferred_element_type=jnp.float32)
        up = jnp.dot(h, wu_ref[:, sl], preferred_element_type=jnp.float32)
        act = (gate * _sigmoid(gate) * up).astype(jnp.bfloat16)
        part = jnp.dot(act, wo_ref[sl, :], preferred_element_type=jnp.float32)
        out = part if out is None else out + part
    o_ref[0] = _layer_norm(ALPHA * x + g2 * out, lng_ref[...], lnb_ref[...])


def _ffn(x, mod, wg, wu, wo, ln_g, ln_b):
    B, L, D = x.shape
    tm = FFN_TM
    const = lambda shape: pl.BlockSpec(shape, lambda b, i: (0,) * len(shape), pipeline_mode=pl.Buffered(1))
    return pl.pallas_call(
        _ffn_kernel,
        out_shape=jax.ShapeDtypeStruct((B, L, D), jnp.float32),
        grid=(B, L // tm),
        in_specs=[
            pl.BlockSpec((1, tm, D), lambda b, i: (b, i, 0)),
            pl.BlockSpec((1, N_MOD, D), lambda b, i: (b, 0, 0)),
            const((D, FFN_HIDDEN)), const((D, FFN_HIDDEN)), const((FFN_HIDDEN, D)),
            const((1, D)), const((1, D)),
        ],
        out_specs=pl.BlockSpec((1, tm, D), lambda b, i: (b, i, 0)),
        compiler_params=_params("parallel", "parallel"),
        name="ffn",
    )(x, mod, wg, wu, wo, ln_g.reshape(1, D), ln_b.reshape(1, D))


def _proj_kinds():
    kinds = []
    for width, kind in ((DIFF_QK_W, ROPE_SCALED), (GATE_W, GATE), (NA_W, SCALED), (DIFF_QK_W, ROPE),
                        (DIFF_V_W + 2 * NA_W, PLAIN)):
        kinds += [kind] * (width // PROJ_TN)
    return tuple(kinds)


def _layer(x, ctx, c, c_ctx, w_mod, b_mod, w_in, b_gate, lam_q1, lam_k1, lam_q2, lam_k2, subln_g,
           na_rpb, w_branch_diff, w_branch_na, w_out, ln1_g, ln1_b, w_ffn_in, w_ffn_out, ln2_g,
           ln2_b, layer_idx):
    B = x.shape[0]
    bf16 = jnp.bfloat16
    lam_init = 0.8 - 0.6 * math.exp(-0.3 * layer_idx)

    mod = _modulation(jnp.concatenate([c, c_ctx[None]], axis=0), w_mod, b_mod)
    mod = mod.reshape(B + 1, N_MOD, D_MODEL)
    mod_x, mod_c = mod[:B], mod[B:]

    cos, sin = _rope_tables()
    w_in_bf = jnp.concatenate([w_in[:, :DIFF_QK_W], w_in[:, Q_COLS:Q_COLS + GATE_W],
                               w_in[:, DIFF_QK_W:Q_COLS], w_in[:, Q_COLS + GATE_W:]], axis=1).astype(bf16)
    bias_x = jnp.zeros((1, IN_W), jnp.float32).at[0, DIFF_QK_W:DIFF_QK_W + GATE_W].set(b_gate)
    px = _projection(x, mod_x, w_in_bf, bias_x, cos, sin, _proj_kinds())
    pc = _projection(ctx, mod_c, w_in_bf[:, Q_COLS + GATE_W:], jnp.zeros((1, KV_COLS), jnp.float32),
                     cos, sin, (PLAIN,) * (KV_COLS // PROJ_TN))

    o_d = _diff_attention(px, pc, lam_q1, lam_k1, lam_q2, lam_k2, subln_g, lam_init)
    o_n = _na_attention(px, pc, _na_bias_tables(na_rpb))

    x_mid = _merge(x, mod_x, o_d, o_n, px, w_branch_diff.astype(bf16), w_branch_na.astype(bf16),
                   w_out.astype(bf16), ln1_g, ln1_b)
    w_ffn_in_bf = w_ffn_in.astype(bf16)
    return _ffn(x_mid, mod_x, w_ffn_in_bf[:, :FFN_HIDDEN], w_ffn_in_bf[:, FFN_HIDDEN:],
                w_ffn_out.astype(bf16), ln2_g, ln2_b)


def kernel(x, c, ctx, c_ctx, w_mod, b_mod, w_in, b_gate, lam_q1, lam_k1, lam_q2, lam_k2, subln_g, na_rpb, w_branch_diff, w_branch_na, w_out, ln1_g, ln1_b, w_ffn_in, w_ffn_out, ln2_g, ln2_b):
    l = 0
    return _layer(x, ctx, c, c_ctx, w_mod[l], b_mod[l], w_in[l], b_gate[l], lam_q1[l], lam_k1[l],
                  lam_q2[l], lam_k2[l], subln_g[l], na_rpb[l], w_branch_diff[l], w_branch_na[l],
                  w_out[l], ln1_g[l], ln1_b[l], w_ffn_in[l], w_ffn_out[l], ln2_g[l], ln2_b[l], l)
```

```python
import functools
import math

import numpy as np
import jax
import jax.numpy as jnp
from jax import lax
from jax.experimental import pallas as pl
from jax.experimental.pallas import tpu as pltpu

D_MODEL = 1024
SEQ = 2048
CTX_LEN = 256
GRID_W = 64
ROWS = SEQ // GRID_W
DEPTH = 1

DIFF_HEADS = 8
DIFF_HD = 64
DIFF_QK_W = DIFF_HEADS * 2 * DIFF_HD
DIFF_V_W = DIFF_HEADS * 2 * DIFF_HD
NA_HEADS = 8
NA_HD = 64
NA_W = NA_HEADS * NA_HD
NA_WIN_H = 8
NA_WIN_W = 16
GATE_W = 2 * D_MODEL
Q_COLS = DIFF_QK_W + NA_W
KV_COLS = DIFF_QK_W + DIFF_V_W + 2 * NA_W
IN_W = Q_COLS + GATE_W + KV_COLS
FFN_HIDDEN = 2816
N_MOD = 6
ROPE_THETA = 10000.0
LN_EPS = 1e-5
ALPHA = (2.0 * DEPTH) ** 0.25

LANES = 128
NEG_BIG = -1e30
VMEM_LIMIT = 56 * 1024 * 1024

QD_BLK = 0
GD_BLK = DIFF_QK_W // LANES
GN_BLK = GD_BLK + D_MODEL // LANES
QN_BLK = GN_BLK + D_MODEL // LANES
KD_BLK = (Q_COLS + GATE_W) // LANES
VD_BLK = KD_BLK + DIFF_QK_W // LANES
KN_BLK = VD_BLK + DIFF_V_W // LANES
VN_BLK = KN_BLK + NA_W // LANES
C_KD_BLK = 0
C_VD_BLK = DIFF_QK_W // LANES
C_KN_BLK = C_VD_BLK + DIFF_V_W // LANES
C_VN_BLK = C_KN_BLK + NA_W // LANES

PLAIN, ROPE_SCALED, SCALED, GATE, ROPE = range(5)
PROJ_TN = 512
PROJ_TM = 1024
QK_SCALE = DIFF_HD ** -0.5


def _sigmoid(x):
    return 1.0 / (1.0 + jnp.exp(-x))


def _params(*sem):
    return pltpu.CompilerParams(dimension_semantics=sem, vmem_limit_bytes=VMEM_LIMIT)


def _mod_kernel(c_ref, w_ref, b_ref, o_ref):
    c = c_ref[...]
    cs = c * _sigmoid(c)
    o_ref[...] = jnp.dot(cs, w_ref[...], preferred_element_type=jnp.float32) + b_ref[...]


def _modulation(cc, w_mod, b_mod):
    n_rows = cc.shape[0]
    n = w_mod.shape[1]
    tn = 1536
    return pl.pallas_call(
        _mod_kernel,
        out_shape=jax.ShapeDtypeStruct((n_rows, n), jnp.float32),
        grid=(n // tn,),
        in_specs=[
            pl.BlockSpec((n_rows, D_MODEL), lambda j: (0, 0)),
            pl.BlockSpec((D_MODEL, tn), lambda j: (0, j)),
            pl.BlockSpec((1, tn), lambda j: (0, j)),
        ],
        out_specs=pl.BlockSpec((n_rows, tn), lambda j: (0, j)),
        compiler_params=_params("parallel"),
        name="modulation",
    )(cc, w_mod, b_mod.reshape(1, n))


def _rope_tables():
    t = jnp.arange(SEQ)
    row = (t // GRID_W).astype(jnp.float32)
    col = (t % GRID_W).astype(jnp.float32)
    n_freq = DIFF_HD // 4
    inv = ROPE_THETA ** (-jnp.arange(n_freq, dtype=jnp.float32) / n_freq)
    ang_r = row[:, None] * inv
    ang_c = col[:, None] * inv
    ang = jnp.concatenate([ang_r, ang_r, ang_c, ang_c], axis=-1)
    sign = jnp.asarray(np.tile(np.repeat([-1.0, 1.0], n_freq), 2), jnp.float32)
    cos = jnp.tile(jnp.cos(ang), (1, LANES // DIFF_HD))
    sin = jnp.tile(jnp.sin(ang) * sign, (1, LANES // DIFF_HD))
    return cos, sin


def _rope_chunk(a, cos, sin_signed, first_half):
    fwd = pltpu.roll(a, LANES - DIFF_HD // 4, 1)
    bwd = pltpu.roll(a, DIFF_HD // 4, 1)
    return a * cos + jnp.where(first_half, fwd, bwd) * sin_signed


def _proj_kernel(kinds, x_ref, mod_ref, w_ref, bias_ref, cos_ref, sin_ref, o_ref, hx_ref):
    j = pl.program_id(2)

    @pl.when(j == 0)
    def _():
        shift = mod_ref[0, 0:1, :]
        scale = mod_ref[0, 1:2, :]
        hx_ref[...] = (x_ref[0] * (1.0 + scale) + shift).astype(jnp.bfloat16)

    acc = jnp.dot(hx_ref[...], w_ref[...], preferred_element_type=jnp.float32)
    n_chunks = acc.shape[1] // LANES

    def rope(scale):
        lane = lax.broadcasted_iota(jnp.int32, (1, LANES), 1)
        first_half = (lane & (DIFF_HD // 4)) == 0
        cos = cos_ref[...]
        sin = sin_ref[...]
        for c in range(n_chunks):
            r = _rope_chunk(acc[:, c * LANES:(c + 1) * LANES], cos, sin, first_half)
            o_ref[0, :, c * LANES:(c + 1) * LANES] = (r * scale).astype(o_ref.dtype)

    for kind in sorted(set(kinds)):
        js = [jj for jj, k in enumerate(kinds) if k == kind]
        lo, hi = js[0], js[-1] + 1
        assert js == list(range(lo, hi))

        @pl.when((j >= lo) & (j < hi))
        def _(kind=kind):
            if kind == PLAIN:
                o_ref[0] = acc.astype(o_ref.dtype)
            elif kind == SCALED:
                o_ref[0] = (acc * QK_SCALE).astype(o_ref.dtype)
            elif kind == GATE:
                o_ref[0] = _sigmoid(acc + bias_ref[...]).astype(o_ref.dtype)
            elif kind == ROPE_SCALED:
                rope(QK_SCALE * LOG2E)
            elif kind == ROPE:
                rope(1.0)


def _projection(x, mod, w, bias, cos, sin, kinds):
    B, L, D = x.shape
    n = w.shape[1]
    tm = min(PROJ_TM, L)
    tn = PROJ_TN
    assert n == tn * len(kinds) and L % tm == 0
    per_batch = mod.shape[0] > 1
    return pl.pallas_call(
        functools.partial(_proj_kernel, kinds),
        out_shape=jax.ShapeDtypeStruct((B, L, n), jnp.bfloat16),
        grid=(B, L // tm, n // tn),
        in_specs=[
            pl.BlockSpec((1, tm, D), lambda b, i, j: (b, i, 0)),
            pl.BlockSpec((1, N_MOD, D), (lambda b, i, j: (b, 0, 0)) if per_batch else (lambda b, i, j: (0, 0, 0))),
            pl.BlockSpec((D, tn), lambda b, i, j: (0, j)),
            pl.BlockSpec((1, tn), lambda b, i, j: (0, j)),
            pl.BlockSpec((tm, LANES), lambda b, i, j: (i, 0)),
            pl.BlockSpec((tm, LANES), lambda b, i, j: (i, 0)),
        ],
        out_specs=pl.BlockSpec((1, tm, tn), lambda b, i, j: (b, i, j)),
        scratch_shapes=[pltpu.VMEM((tm, D), jnp.bfloat16)],
        compiler_params=_params("parallel", "parallel", "arbitrary"),
        name="projection",
    )(x, mod, w, bias, cos, sin)


DIFF_TQ = 512
DIFF_TQ_SUB = 256
DIFF_TK = 256
LOG2E = math.log2(math.e)


def _nt_dot(a, b):
    return lax.dot_general(a, b, (((1,), (1,)), ((), ())), preferred_element_type=jnp.float32)


def _diff_attn_kernel(lam_init, lq1_ref, lk1_ref, lq2_ref, lk2_ref, g_ref,
                      q_ref, kx_ref, vx_ref, kc_ref, vc_ref, o_ref, vt_ref, s_ref):
    @pl.when(pl.program_id(2) == 0)
    def _():
        vt_ref[:, :SEQ] = vx_ref[0].astype(jnp.float32).T.astype(jnp.bfloat16)
        vt_ref[:, SEQ:] = vc_ref[0].astype(jnp.float32).T.astype(jnp.bfloat16)

    q = q_ref[0]
    lane = lax.broadcasted_iota(jnp.int32, (1, LANES), 1)
    zero = jnp.zeros_like(q)
    q_parts = (jnp.where(lane < DIFF_HD, q, zero), jnp.where(lane >= DIFF_HD, q, zero))

    tq = q.shape[0]
    n_tiles = (SEQ + CTX_LEN) // DIFF_TK
    sub = 8

    def key_tile(t):
        if t < SEQ // DIFF_TK:
            return kx_ref[0, t * DIFF_TK:(t + 1) * DIFF_TK, :]
        return kc_ref[0, (t - SEQ // DIFF_TK) * DIFF_TK:(t + 1 - SEQ // DIFF_TK) * DIFF_TK, :]

    maxes = []
    for c, qp in enumerate(q_parts):
        m8 = None
        for t in range(n_tiles):
            s = _nt_dot(key_tile(t), qp)
            s_ref[c, t * DIFF_TK:(t + 1) * DIFF_TK, :] = s
            mt = jnp.max(s.reshape(DIFF_TK // sub, sub, tq), axis=0)
            m8 = mt if m8 is None else jnp.maximum(m8, mt)
        maxes.append(jnp.max(m8, axis=0, keepdims=True))

    outs = []
    for c in range(2):
        halves = []
        for h0 in range(0, tq, DIFF_TQ_SUB):
            m = maxes[c][:, h0:h0 + DIFF_TQ_SUB]
            l8 = acc = None
            for t in range(n_tiles):
                e = jnp.exp2(s_ref[c, t * DIFF_TK:(t + 1) * DIFF_TK, h0:h0 + DIFF_TQ_SUB] - m)
                lt = jnp.sum(e.reshape(DIFF_TK // sub, sub, DIFF_TQ_SUB), axis=0)
                pv = jnp.dot(vt_ref[:, t * DIFF_TK:(t + 1) * DIFF_TK], e.astype(jnp.bfloat16),
                             preferred_element_type=jnp.float32)
                l8 = lt if l8 is None else l8 + lt
                acc = pv if acc is None else acc + pv
            halves.append(acc * (1.0 / jnp.sum(l8, axis=0, keepdims=True)))
        outs.append(jnp.concatenate(halves, axis=1))

    lam = (jnp.exp(jnp.sum(lq1_ref[...] * lk1_ref[...], axis=-1, keepdims=True))
           - jnp.exp(jnp.sum(lq2_ref[...] * lk2_ref[...], axis=-1, keepdims=True)) + lam_init)
    o = outs[0] - lam * outs[1]
    y = (o * lax.rsqrt(jnp.mean(o * o, axis=0, keepdims=True) + LN_EPS)).T
    o_ref[0] = (y * g_ref[...] * (1.0 - lam_init)).astype(o_ref.dtype)


def _diff_attention(px, pc, lam_q1, lam_k1, lam_q2, lam_k2, subln_g, lam_init):
    B = px.shape[0]
    tq = DIFF_TQ
    vec = lambda a: a.reshape(1, -1)
    small = lambda n: pl.BlockSpec((1, n), lambda b, h, i: (0, 0))
    return pl.pallas_call(
        functools.partial(_diff_attn_kernel, lam_init),
        out_shape=jax.ShapeDtypeStruct((B, SEQ, DIFF_V_W), jnp.bfloat16),
        grid=(B, DIFF_HEADS, SEQ // tq),
        in_specs=[
            small(DIFF_HD), small(DIFF_HD), small(DIFF_HD), small(DIFF_HD), small(2 * DIFF_HD),
            pl.BlockSpec((1, tq, LANES), lambda b, h, i: (b, i, QD_BLK + h)),
            pl.BlockSpec((1, SEQ, LANES), lambda b, h, i: (b, 0, KD_BLK + h)),
            pl.BlockSpec((1, SEQ, LANES), lambda b, h, i: (b, 0, VD_BLK + h)),
            pl.BlockSpec((1, CTX_LEN, LANES), lambda b, h, i: (b, 0, C_KD_BLK + h)),
            pl.BlockSpec((1, CTX_LEN, LANES), lambda b, h, i: (b, 0, C_VD_BLK + h)),
        ],
        out_specs=pl.BlockSpec((1, tq, LANES), lambda b, h, i: (b, i, h)),
        scratch_shapes=[pltpu.VMEM((LANES, SEQ + CTX_LEN), jnp.bfloat16),
                        pltpu.VMEM((2, SEQ + CTX_LEN, tq), jnp.float32)],
        compiler_params=_params("parallel", "parallel", "arbitrary"),
        name="diff_attention",
    )(vec(lam_q1), vec(lam_k1), vec(lam_q2), vec(lam_k2), vec(subln_g), px, px, px, pc, pc)


NA_QROWS = 8
NA_KROWS = 16
NA_TQ = NA_QROWS * GRID_W
NA_TK = NA_KROWS * GRID_W
NA_BLOCKS = ROWS // NA_QROWS


def _na_key_row_start(i):
    if isinstance(i, int):
        return min(max(i * NA_QROWS - NA_WIN_H // 2, 0), ROWS - NA_KROWS)
    return jnp.clip(i * NA_QROWS - NA_WIN_H // 2, 0, ROWS - NA_KROWS)


def _na_bias_tables(rpb):
    wh = min(NA_WIN_H, ROWS)
    col = np.arange(GRID_W)
    col_start = np.clip(col - NA_WIN_W // 2, 0, GRID_W - NA_WIN_W)
    col_ok = (col[None, :] >= col_start[:, None]) & (col[None, :] < col_start[:, None] + NA_WIN_W)
    col_idx = np.clip(col[None, :] - col[:, None] + NA_WIN_W - 1, 0, 2 * NA_WIN_W - 2)
    dr_idx = np.zeros((NA_BLOCKS, NA_QROWS, NA_KROWS), np.int32)
    row_ok = np.zeros((NA_BLOCKS, NA_QROWS, NA_KROWS), bool)
    for i in range(NA_BLOCKS):
        ks = _na_key_row_start(i)
        for a in range(NA_QROWS):
            qr = i * NA_QROWS + a
            rs = min(max(qr - wh // 2, 0), ROWS - wh)
            for c in range(NA_KROWS):
                kr = ks + c
                row_ok[i, a, c] = rs <= kr < rs + wh
                dr_idx[i, a, c] = min(max(kr - qr + NA_WIN_H - 1, 0), 2 * NA_WIN_H - 2)
    r = rpb[:, :, col_idx]
    t = r[:, dr_idx]
    ok = row_ok[:, :, :, None, None] & col_ok[None, None, None]
    t = jnp.where(ok[None], t, NEG_BIG)
    t = jnp.transpose(t, (0, 1, 2, 4, 3, 5))
    return t.reshape(NA_HEADS, NA_BLOCKS, NA_TQ, NA_TK).astype(jnp.float32)


def _na_kernel(bias_ref, q_ref, k_ref, v_ref, kc_ref, vc_ref, o_ref):
    i = pl.program_id(1)
    start = pl.multiple_of(_na_key_row_start(i) * GRID_W, 4 * GRID_W)
    q = q_ref[0]
    k = k_ref[0, pl.ds(start, NA_TK), :]
    v = v_ref[0, pl.ds(start, NA_TK), :]
    kc = kc_ref[0]
    vc = vc_ref[0]
    lane = lax.broadcasted_iota(jnp.int32, (1, LANES), 1)
    zero = jnp.zeros_like(q)
    outs = []
    for hh in range(2):
        in_head = (lane >= hh * NA_HD) & (lane < (hh + 1) * NA_HD)
        qh = jnp.where(in_head, q, zero)
        s_loc = _nt_dot(qh, k) + bias_ref[hh, 0]
        s_ctx = _nt_dot(qh, kc)
        m = jnp.maximum(jnp.max(s_loc, axis=-1, keepdims=True), jnp.max(s_ctx, axis=-1, keepdims=True))
        e_loc = jnp.exp(s_loc - m)
        e_ctx = jnp.exp(s_ctx - m)
        l = jnp.sum(e_loc, axis=-1, keepdims=True) + jnp.sum(e_ctx, axis=-1, keepdims=True)
        o = (jnp.dot(e_loc.astype(jnp.bfloat16), v, preferred_element_type=jnp.float32)
             + jnp.dot(e_ctx.astype(jnp.bfloat16), vc, preferred_element_type=jnp.float32))
        outs.append(o * (1.0 / l))
    o_ref[0] = jnp.where(lane < NA_HD, outs[0], outs[1]).astype(o_ref.dtype)


def _na_attention(px, pc, bias):
    B = px.shape[0]
    return pl.pallas_call(
        _na_kernel,
        out_shape=jax.ShapeDtypeStruct((B, SEQ, NA_W), jnp.bfloat16),
        grid=(NA_HEADS // 2, NA_BLOCKS, B),
        in_specs=[
            pl.BlockSpec((2, 1, NA_TQ, NA_TK), lambda hp, i, b: (hp, i, 0, 0)),
            pl.BlockSpec((1, NA_TQ, LANES), lambda hp, i, b: (b, i, QN_BLK + hp)),
            pl.BlockSpec((1, SEQ, LANES), lambda hp, i, b: (b, 0, KN_BLK + hp)),
            pl.BlockSpec((1, SEQ, LANES), lambda hp, i, b: (b, 0, VN_BLK + hp)),
            pl.BlockSpec((1, CTX_LEN, LANES), lambda hp, i, b: (b, 0, C_KN_BLK + hp)),
            pl.BlockSpec((1, CTX_LEN, LANES), lambda hp, i, b: (b, 0, C_VN_BLK + hp)),
        ],
        out_specs=pl.BlockSpec((1, NA_TQ, LANES), lambda hp, i, b: (b, i, hp)),
        compiler_params=_params("parallel", "parallel", "arbitrary"),
        name="na_attention",
    )(bias, px, px, px, pc, pc)


def _layer_norm(z, g, b):
    mu = jnp.mean(z, axis=-1, keepdims=True)
    zc = z - mu
    var = jnp.mean(zc * zc, axis=-1, keepdims=True)
    return zc * lax.rsqrt(var + LN_EPS) * g + b


MERGE_TM = 512


def _merge_kernel(x_ref, mod_ref, od_ref, on_ref, gd_ref, gn_ref, wbd_ref, wbn_ref, wo_ref,
                  lng_ref, lnb_ref, o_ref):
    yd = jnp.dot(od_ref[0], wbd_ref[...], preferred_element_type=jnp.float32)
    yn = jnp.dot(on_ref[0], wbn_ref[...], preferred_element_type=jnp.float32)
    y = gd_ref[0].astype(jnp.float32) * yd + gn_ref[0].astype(jnp.float32) * yn
    mix = jnp.dot(y.astype(jnp.bfloat16), wo_ref[...], preferred_element_type=jnp.float32)
    g1 = mod_ref[0, 2:3, :]
    o_ref[0] = _layer_norm(ALPHA * x_ref[0] + g1 * mix, lng_ref[...], lnb_ref[...])


def _merge(x, mod, o_d, o_n, px, wbd, wbn, wo, ln_g, ln_b):
    B, L, D = x.shape
    tm = MERGE_TM
    gblk = D // LANES
    const = lambda shape: pl.BlockSpec(shape, lambda b, i: (0,) * len(shape))
    return pl.pallas_call(
        _merge_kernel,
        out_shape=jax.ShapeDtypeStruct((B, L, D), jnp.float32),
        grid=(B, L // tm),
        in_specs=[
            pl.BlockSpec((1, tm, D), lambda b, i: (b, i, 0)),
            pl.BlockSpec((1, N_MOD, D), lambda b, i: (b, 0, 0)),
            pl.BlockSpec((1, tm, DIFF_V_W), lambda b, i: (b, i, 0)),
            pl.BlockSpec((1, tm, NA_W), lambda b, i: (b, i, 0)),
            pl.BlockSpec((1, tm, D), lambda b, i: (b, i, GD_BLK // gblk)),
            pl.BlockSpec((1, tm, D), lambda b, i: (b, i, GN_BLK // gblk)),
            const((DIFF_V_W, D)), const((NA_W, D)), const((D, D)), const((1, D)), const((1, D)),
        ],
        out_specs=pl.BlockSpec((1, tm, D), lambda b, i: (b, i, 0)),
        compiler_params=_params("parallel", "parallel"),
        name="merge",
    )(x, mod, o_d, o_n, px, px, wbd, wbn, wo, ln_g.reshape(1, D), ln_b.reshape(1, D))


FFN_TM = 512
FFN_CHUNK = FFN_HIDDEN // 2


def _ffn_kernel(x_ref, mod_ref, wg_ref, wu_ref, wo_ref, lng_ref, lnb_ref, o_ref):
    x = x_ref[0]
    shift = mod_ref[0, 3:4, :]
    scale = mod_ref[0, 4:5, :]
    g2 = mod_ref[0, 5:6, :]
    h = (x * (1.0 + scale) + shift).astype(jnp.bfloat16)
    out = None
    for c in range(FFN_HIDDEN // FFN_CHUNK):
        sl = slice(c * FFN_CHUNK, (c + 1) * FFN_CHUNK)
        gate = jnp.dot(h, wg_ref[:, sl], preferred_element_type=jnp.float32)
        up = jnp.dot(h, wu_ref[:, sl], preferred_element_type=jnp.float32)
        act = (gate * _sigmoid(gate) * up).astype(jnp.bfloat16)
        part = jnp.dot(act, wo_ref[sl, :], preferred_element_type=jnp.float32)
        out = part if out is None else out + part
    o_ref[0] = _layer_norm(ALPHA * x + g2 * out, lng_ref[...], lnb_ref[...])


def _ffn(x, mod, wg, wu, wo, ln_g, ln_b):
    B, L, D = x.shape
    tm = FFN_TM
    const = lambda shape: pl.BlockSpec(shape, lambda b, i: (0,) * len(shape), pipeline_mode=pl.Buffered(1))
    return pl.pallas_call(
        _ffn_kernel,
        out_shape=jax.ShapeDtypeStruct((B, L, D), jnp.float32),
        grid=(B, L // tm),
        in_specs=[
            pl.BlockSpec((1, tm, D), lambda b, i: (b, i, 0)),
            pl.BlockSpec((1, N_MOD, D), lambda b, i: (b, 0, 0)),
            const((D, FFN_HIDDEN)), const((D, FFN_HIDDEN)), const((FFN_HIDDEN, D)),
            const((1, D)), const((1, D)),
        ],
        out_specs=pl.BlockSpec((1, tm, D), lambda b, i: (b, i, 0)),
        compiler_params=_params("parallel", "parallel"),
        name="ffn",
    )(x, mod, wg, wu, wo, ln_g.reshape(1, D), ln_b.reshape(1, D))


def _proj_kinds():
    kinds = []
    for width, kind in ((DIFF_QK_W, ROPE_SCALED), (GATE_W, GATE), (NA_W, SCALED), (DIFF_QK_W, ROPE),
                        (DIFF_V_W + 2 * NA_W, PLAIN)):
        kinds += [kind] * (width // PROJ_TN)
    return tuple(kinds)


def _layer(x, ctx, c, c_ctx, w_mod, b_mod, w_in, b_gate, lam_q1, lam_k1, lam_q2, lam_k2, subln_g,
           na_rpb, w_branch_diff, w_branch_na, w_out, ln1_g, ln1_b, w_ffn_in, w_ffn_out, ln2_g,
           ln2_b, layer_idx):
    B = x.shape[0]
    bf16 = jnp.bfloat16
    lam_init = 0.8 - 0.6 * math.exp(-0.3 * layer_idx)

    mod = _modulation(jnp.concatenate([c, c_ctx[None]], axis=0), w_mod, b_mod)
    mod = mod.reshape(B + 1, N_MOD, D_MODEL)
    mod_x, mod_c = mod[:B], mod[B:]

    cos, sin = _rope_tables()
    w_in_bf = jnp.concatenate([w_in[:, :DIFF_QK_W], w_in[:, Q_COLS:Q_COLS + GATE_W],
                               w_in[:, DIFF_QK_W:Q_COLS], w_in[:, Q_COLS + GATE_W:]], axis=1).astype(bf16)
    bias_x = jnp.zeros((1, IN_W), jnp.float32).at[0, DIFF_QK_W:DIFF_QK_W + GATE_W].set(b_gate)
    px = _projection(x, mod_x, w_in_bf, bias_x, cos, sin, _proj_kinds())
    pc = _projection(ctx, mod_c, w_in_bf[:, Q_COLS + GATE_W:], jnp.zeros((1, KV_COLS), jnp.float32),
                     cos, sin, (PLAIN,) * (KV_COLS // PROJ_TN))

    o_d = _diff_attention(px, pc, lam_q1, lam_k1, lam_q2, lam_k2, subln_g, lam_init)
    o_n = _na_attention(px, pc, _na_bias_tables(na_rpb))

    x_mid = _merge(x, mod_x, o_d, o_n, px, w_branch_diff.astype(bf16), w_branch_na.astype(bf16),
                   w_out.astype(bf16), ln1_g, ln1_b)
    w_ffn_in_bf = w_ffn_in.astype(bf16)
    return _ffn(x_mid, mod_x, w_ffn_in_bf[:, :FFN_HIDDEN], w_ffn_in_bf[:, FFN_HIDDEN:],
                w_ffn_out.astype(bf16), ln2_g, ln2_b)


def kernel(x, c, ctx, c_ctx, w_mod, b_mod, w_in, b_gate, lam_q1, lam_k1, lam_q2, lam_k2, subln_g, na_rpb, w_branch_diff, w_branch_na, w_out, ln1_g, ln1_b, w_ffn_in, w_ffn_out, ln2_g, ln2_b):
    l = 0
    return _layer(x, ctx, c, c_ctx, w_mod[l], b_mod[l], w_in[l], b_gate[l], lam_q1[l], lam_k1[l],
                  lam_q2[l], lam_k2[l], subln_g[l], na_rpb[l], w_branch_diff[l], w_branch_na[l],
                  w_out[l], ln1_g[l], ln1_b[l], w_ffn_in[l], w_ffn_out[l], ln2_g[l], ln2_b[l], l)
```

```python
import functools
import math

import numpy as np
import jax
import jax.numpy as jnp
from jax import lax
from jax.experimental import pallas as pl
from jax.experimental.pallas import tpu as pltpu

D_MODEL = 1024
SEQ = 2048
CTX_LEN = 256
GRID_W = 64
ROWS = SEQ // GRID_W
DEPTH = 1

DIFF_HEADS = 8
DIFF_HD = 64
DIFF_QK_W = DIFF_HEADS * 2 * DIFF_HD
DIFF_V_W = DIFF_HEADS * 2 * DIFF_HD
NA_HEADS = 8
NA_HD = 64
NA_W = NA_HEADS * NA_HD
NA_WIN_H = 8
NA_WIN_W = 16
GATE_W = 2 * D_MODEL
Q_COLS = DIFF_QK_W + NA_W
KV_COLS = DIFF_QK_W + DIFF_V_W + 2 * NA_W
IN_W = Q_COLS + GATE_W + KV_COLS
FFN_HIDDEN = 2816
N_MOD = 6
ROPE_THETA = 10000.0
LN_EPS = 1e-5
ALPHA = (2.0 * DEPTH) ** 0.25

LANES = 128
NEG_BIG = -1e30
VMEM_LIMIT = 56 * 1024 * 1024

QD_BLK = 0
GD_BLK = DIFF_QK_W // LANES
GN_BLK = GD_BLK + D_MODEL // LANES
QN_BLK = GN_BLK + D_MODEL // LANES
KD_BLK = (Q_COLS + GATE_W) // LANES
VD_BLK = KD_BLK + DIFF_QK_W // LANES
KN_BLK = VD_BLK + DIFF_V_W // LANES
VN_BLK = KN_BLK + NA_W // LANES
C_KD_BLK = 0
C_VD_BLK = DIFF_QK_W // LANES
C_KN_BLK = C_VD_BLK + DIFF_V_W // LANES
C_VN_BLK = C_KN_BLK + NA_W // LANES

PLAIN, ROPE_SCALED, SCALED, GATE, ROPE = range(5)
PROJ_TN = 512
PROJ_TM = 512
QK_SCALE = DIFF_HD ** -0.5
LOG2E = math.log2(math.e)


def _sigmoid(x):
    return 1.0 / (1.0 + jnp.exp(-x))


def _params(*sem):
    return pltpu.CompilerParams(dimension_semantics=sem, vmem_limit_bytes=VMEM_LIMIT)


def _mod_kernel(c_ref, w_ref, b_ref, o_ref):
    c = c_ref[...]
    cs = c * _sigmoid(c)
    o_ref[...] = jnp.dot(cs, w_ref[...], preferred_element_type=jnp.float32) + b_ref[...]


def _modulation(cc, w_mod, b_mod):
    n_rows = cc.shape[0]
    n = w_mod.shape[1]
    tn = 1536
    return pl.pallas_call(
        _mod_kernel,
        out_shape=jax.ShapeDtypeStruct((n_rows, n), jnp.float32),
        grid=(n // tn,),
        in_specs=[
            pl.BlockSpec((n_rows, D_MODEL), lambda j: (0, 0)),
            pl.BlockSpec((D_MODEL, tn), lambda j: (0, j)),
            pl.BlockSpec((1, tn), lambda j: (0, j)),
        ],
        out_specs=pl.BlockSpec((n_rows, tn), lambda j: (0, j)),
        compiler_params=_params("parallel"),
        name="modulation",
    )(cc, w_mod, b_mod.reshape(1, n))


def _rope_tables():
    t = jnp.arange(SEQ)
    row = (t // GRID_W).astype(jnp.float32)
    col = (t % GRID_W).astype(jnp.float32)
    n_freq = DIFF_HD // 4
    inv = ROPE_THETA ** (-jnp.arange(n_freq, dtype=jnp.float32) / n_freq)
    ang = jnp.concatenate([row[:, None] * inv, col[:, None] * inv], axis=-1)
    reps = LANES // (2 * n_freq)
    return jnp.tile(jnp.cos(ang), (1, reps)), jnp.tile(jnp.sin(ang), (1, reps))


def _rope_column_order(w):
    d = w.shape[0]
    w = w.reshape(d, DIFF_HEADS // 2, 4, 2, 2, DIFF_HD // 4)
    return jnp.transpose(w, (0, 1, 4, 2, 3, 5)).reshape(d, DIFF_QK_W)


def _proj_kernel(kinds, x_ref, mod_ref, w_ref, bias_ref, cos_ref, sin_ref, o_ref):
    shift = mod_ref[0, 0:1, :]
    scale = mod_ref[0, 1:2, :]
    hx = (x_ref[0] * (1.0 + scale) + shift).astype(jnp.bfloat16)
    for j, kind in enumerate(kinds):
        lo = j * PROJ_TN
        acc = jnp.dot(hx, w_ref[:, lo:lo + PROJ_TN], preferred_element_type=jnp.float32)
        if kind == PLAIN:
            o_ref[0, :, lo:lo + PROJ_TN] = acc.astype(o_ref.dtype)
        elif kind == SCALED:
            o_ref[0, :, lo:lo + PROJ_TN] = (acc * QK_SCALE).astype(o_ref.dtype)
        elif kind == GATE:
            o_ref[0, :, lo:lo + PROJ_TN] = _sigmoid(acc + bias_ref[:, lo:lo + PROJ_TN]).astype(o_ref.dtype)
        else:
            out_scale = QK_SCALE * LOG2E if kind == ROPE_SCALED else 1.0
            cos = cos_ref[...] * out_scale
            sin = sin_ref[...] * out_scale
            for p in range(PROJ_TN // (2 * LANES)):
                x1 = acc[:, 2 * p * LANES:(2 * p + 1) * LANES]
                x2 = acc[:, (2 * p + 1) * LANES:(2 * p + 2) * LANES]
                o_ref[0, :, lo + 2 * p * LANES:lo + (2 * p + 1) * LANES] = (x1 * cos - x2 * sin).astype(o_ref.dtype)
                o_ref[0, :, lo + (2 * p + 1) * LANES:lo + (2 * p + 2) * LANES] = (x2 * cos + x1 * sin).astype(o_ref.dtype)


def _projection(x, mod, w, bias, cos, sin, kinds):
    B, L, D = x.shape
    n = w.shape[1]
    tm = min(PROJ_TM, L)
    assert n == PROJ_TN * len(kinds) and L % tm == 0
    per_batch = mod.shape[0] > 1
    resident = lambda shape: pl.BlockSpec(shape, lambda b, i: (0, 0), pipeline_mode=pl.Buffered(1))
    return pl.pallas_call(
        functools.partial(_proj_kernel, kinds),
        out_shape=jax.ShapeDtypeStruct((B, L, n), jnp.bfloat16),
        grid=(B, L // tm),
        in_specs=[
            pl.BlockSpec((1, tm, D), lambda b, i: (b, i, 0)),
            pl.BlockSpec((1, N_MOD, D), (lambda b, i: (b, 0, 0)) if per_batch else (lambda b, i: (0, 0, 0))),
            resident((D, n)),
            resident((1, n)),
            pl.BlockSpec((tm, LANES), lambda b, i: (i, 0)),
            pl.BlockSpec((tm, LANES), lambda b, i: (i, 0)),
        ],
        out_specs=pl.BlockSpec((1, tm, n), lambda b, i: (b, i, 0)),
        compiler_params=_params("parallel", "parallel"),
        name="projection",
    )(x, mod, w, bias, cos, sin)


DIFF_TQ = 512
DIFF_TQ_SUB = 256
DIFF_TK = 256
PAIR_W = 2 * LANES
SUBLANES = 8


def _nt_dot(a, b):
    return lax.dot_general(a, b, (((1,), (1,)), ((), ())), preferred_element_type=jnp.float32)


def _diff_attn_kernel(lam_init, lq1_ref, lk1_ref, lq2_ref, lk2_ref, g_ref,
                      q_ref, kx_ref, vx_ref, kc_ref, vc_ref, o_ref, vt_ref, s_ref):
    vt_ref[:, :SEQ] = vx_ref[0].astype(jnp.float32).T.astype(jnp.bfloat16)
    vt_ref[:, SEQ:] = vc_ref[0].astype(jnp.float32).T.astype(jnp.bfloat16)

    tq = DIFF_TQ
    n_q = SEQ // tq
    n_tiles = (SEQ + CTX_LEN) // DIFF_TK
    lane = lax.broadcasted_iota(jnp.int32, (1, PAIR_W), 1)
    group = (lane & (LANES - 1)) >> 5
    first_group = 2 * (pl.program_id(1) % 2)
    lam = (jnp.exp(jnp.sum(lq1_ref[...] * lk1_ref[...], axis=-1, keepdims=True))
           - jnp.exp(jnp.sum(lq2_ref[...] * lk2_ref[...], axis=-1, keepdims=True)) + lam_init)

    def key_tile(t):
        if t < SEQ // DIFF_TK:
            return kx_ref[0, t * DIFF_TK:(t + 1) * DIFF_TK, :]
        return kc_ref[0, (t - SEQ // DIFF_TK) * DIFF_TK:(t + 1 - SEQ // DIFF_TK) * DIFF_TK, :]

    def pass1(qi):
        q = q_ref[0, qi * tq:(qi + 1) * tq, :]
        zero = jnp.zeros_like(q)
        maxes = []
        for c in range(2):
            qp = jnp.where(group == first_group + c, q, zero)
            m8 = None
            for t in range(n_tiles):
                s = _nt_dot(key_tile(t), qp)
                s_ref[qi % 2, c, t * DIFF_TK:(t + 1) * DIFF_TK, :] = s
                mt = jnp.max(s.reshape(DIFF_TK // SUBLANES, SUBLANES, tq), axis=0)
                m8 = mt if m8 is None else jnp.maximum(m8, mt)
            maxes.append(jnp.max(m8, axis=0, keepdims=True))
        return maxes

    def pass2(qi, maxes):
        outs = []
        for c in range(2):
            halves = []
            for h0 in range(0, tq, DIFF_TQ_SUB):
                m = maxes[c][:, h0:h0 + DIFF_TQ_SUB]
                l8 = acc = None
                for t in range(n_tiles):
                    e = jnp.exp2(s_ref[qi % 2, c, t * DIFF_TK:(t + 1) * DIFF_TK, h0:h0 + DIFF_TQ_SUB] - m)
                    lt = jnp.sum(e.reshape(DIFF_TK // SUBLANES, SUBLANES, DIFF_TQ_SUB), axis=0)
                    pv = jnp.dot(vt_ref[:, t * DIFF_TK:(t + 1) * DIFF_TK], e.astype(jnp.bfloat16),
                                 preferred_element_type=jnp.float32)
                    l8 = lt if l8 is None else l8 + lt
                    acc = pv if acc is None else acc + pv
                halves.append(acc * (1.0 / jnp.sum(l8, axis=0, keepdims=True)))
            outs.append(jnp.concatenate(halves, axis=1))
        o = outs[0] - lam * outs[1]
        y = (o * lax.rsqrt(jnp.mean(o * o, axis=0, keepdims=True) + LN_EPS)).T
        o_ref[0, qi * tq:(qi + 1) * tq, :] = (y * g_ref[...] * (1.0 - lam_init)).astype(o_ref.dtype)

    prev = None
    for qi in range(n_q):
        cur = pass1(qi)
        if prev is not None:
            pass2(qi - 1, prev)
        prev = cur
    pass2(n_q - 1, prev)


def _diff_attention(px, pc, lam_q1, lam_k1, lam_q2, lam_k2, subln_g, lam_init):
    B = px.shape[0]
    vec = lambda a: a.reshape(1, -1)
    small = lambda n: pl.BlockSpec((1, n), lambda b, h: (0, 0))
    qd_pair = QD_BLK * LANES // PAIR_W
    kd_pair = KD_BLK * LANES // PAIR_W
    c_kd_pair = C_KD_BLK * LANES // PAIR_W
    return pl.pallas_call(
        functools.partial(_diff_attn_kernel, lam_init),
        out_shape=jax.ShapeDtypeStruct((B, SEQ, DIFF_V_W), jnp.bfloat16),
        grid=(B, DIFF_HEADS),
        in_specs=[
            small(DIFF_HD), small(DIFF_HD), small(DIFF_HD), small(DIFF_HD), small(2 * DIFF_HD),
            pl.BlockSpec((1, SEQ, PAIR_W), lambda b, h: (b, 0, qd_pair + h // 2)),
            pl.BlockSpec((1, SEQ, PAIR_W), lambda b, h: (b, 0, kd_pair + h // 2)),
            pl.BlockSpec((1, SEQ, LANES), lambda b, h: (b, 0, VD_BLK + h)),
            pl.BlockSpec((1, CTX_LEN, PAIR_W), lambda b, h: (b, 0, c_kd_pair + h // 2)),
            pl.BlockSpec((1, CTX_LEN, LANES), lambda b, h: (b, 0, C_VD_BLK + h)),
        ],
        out_specs=pl.BlockSpec((1, SEQ, LANES), lambda b, h: (b, 0, h)),
        scratch_shapes=[pltpu.VMEM((LANES, SEQ + CTX_LEN), jnp.bfloat16),
                        pltpu.VMEM((2, 2, SEQ + CTX_LEN, DIFF_TQ), jnp.float32)],
        compiler_params=_params("parallel", "parallel"),
        name="diff_attention",
    )(vec(lam_q1), vec(lam_k1), vec(lam_q2), vec(lam_k2), vec(subln_g), px, px, px, pc, pc)


NA_QROWS = 8
NA_KROWS = 16
NA_TQ = NA_QROWS * GRID_W
NA_TK = NA_KROWS * GRID_W
NA_BLOCKS = ROWS // NA_QROWS


def _na_key_row_start(i):
    if isinstance(i, int):
        return min(max(i * NA_QROWS - NA_WIN_H // 2, 0), ROWS - NA_KROWS)
    return jnp.clip(i * NA_QROWS - NA_WIN_H // 2, 0, ROWS - NA_KROWS)


def _na_bias_tables(rpb):
    wh = min(NA_WIN_H, ROWS)
    col = np.arange(GRID_W)
    col_start = np.clip(col - NA_WIN_W // 2, 0, GRID_W - NA_WIN_W)
    col_ok = (col[None, :] >= col_start[:, None]) & (col[None, :] < col_start[:, None] + NA_WIN_W)
    col_idx = np.clip(col[None, :] - col[:, None] + NA_WIN_W - 1, 0, 2 * NA_WIN_W - 2)
    dr_idx = np.zeros((NA_BLOCKS, NA_QROWS, NA_KROWS), np.int32)
    row_ok = np.zeros((NA_BLOCKS, NA_QROWS, NA_KROWS), bool)
    for i in range(NA_BLOCKS):
        ks = _na_key_row_start(i)
        for a in range(NA_QROWS):
            qr = i * NA_QROWS + a
            rs = min(max(qr - wh // 2, 0), ROWS - wh)
            for c in range(NA_KROWS):
                kr = ks + c
                row_ok[i, a, c] = rs <= kr < rs + wh
                dr_idx[i, a, c] = min(max(kr - qr + NA_WIN_H - 1, 0), 2 * NA_WIN_H - 2)
    r = rpb[:, :, col_idx]
    t = r[:, dr_idx]
    ok = row_ok[:, :, :, None, None] & col_ok[None, None, None]
    t = jnp.where(ok[None], t, NEG_BIG)
    t = jnp.transpose(t, (0, 1, 2, 4, 3, 5))
    return t.reshape(NA_HEADS, NA_BLOCKS, NA_TQ, NA_TK).astype(jnp.float32)


def _na_kernel(bias_ref, q_ref, k_ref, v_ref, kc_ref, vc_ref, o_ref):
    i = pl.program_id(1)
    start = pl.multiple_of(_na_key_row_start(i) * GRID_W, 4 * GRID_W)
    q = q_ref[0]
    k = k_ref[0, pl.ds(start, NA_TK), :]
    v = v_ref[0, pl.ds(start, NA_TK), :]
    kc = kc_ref[0]
    vc = vc_ref[0]
    lane = lax.broadcasted_iota(jnp.int32, (1, LANES), 1)
    zero = jnp.zeros_like(q)
    outs = []
    for hh in range(2):
        in_head = (lane >= hh * NA_HD) & (lane < (hh + 1) * NA_HD)
        qh = jnp.where(in_head, q, zero)
        s_loc = _nt_dot(qh, k) + bias_ref[hh, 0]
        s_ctx = _nt_dot(qh, kc)
        m = jnp.maximum(jnp.max(s_loc, axis=-1, keepdims=True), jnp.max(s_ctx, axis=-1, keepdims=True))
        e_loc = jnp.exp(s_loc - m)
        e_ctx = jnp.exp(s_ctx - m)
        l = jnp.sum(e_loc, axis=-1, keepdims=True) + jnp.sum(e_ctx, axis=-1, keepdims=True)
        o = (jnp.dot(e_loc.astype(jnp.bfloat16), v, preferred_element_type=jnp.float32)
             + jnp.dot(e_ctx.astype(jnp.bfloat16), vc, preferred_element_type=jnp.float32))
        outs.append(o * (1.0 / l))
    o_ref[0] = jnp.where(lane < NA_HD, outs[0], outs[1]).astype(o_ref.dtype)


def _na_attention(px, pc, bias):
    B = px.shape[0]
    return pl.pallas_call(
        _na_kernel,
        out_shape=jax.ShapeDtypeStruct((B, SEQ, NA_W), jnp.bfloat16),
        grid=(NA_HEADS // 2, NA_BLOCKS, B),
        in_specs=[
            pl.BlockSpec((2, 1, NA_TQ, NA_TK), lambda hp, i, b: (hp, i, 0, 0)),
            pl.BlockSpec((1, NA_TQ, LANES), lambda hp, i, b: (b, i, QN_BLK + hp)),
            pl.BlockSpec((1, SEQ, LANES), lambda hp, i, b: (b, 0, KN_BLK + hp)),
            pl.BlockSpec((1, SEQ, LANES), lambda hp, i, b: (b, 0, VN_BLK + hp)),
            pl.BlockSpec((1, CTX_LEN, LANES), lambda hp, i, b: (b, 0, C_KN_BLK + hp)),
            pl.BlockSpec((1, CTX_LEN, LANES), lambda hp, i, b: (b, 0, C_VN_BLK + hp)),
        ],
        out_specs=pl.BlockSpec((1, NA_TQ, LANES), lambda hp, i, b: (b, i, hp)),
        compiler_params=_params("parallel", "parallel", "arbitrary"),
        name="na_attention",
    )(bias, px, px, px, pc, pc)


def _layer_norm(z, g, b):
    mu = jnp.mean(z, axis=-1, keepdims=True)
    zc = z - mu
    var = jnp.mean(zc * zc, axis=-1, keepdims=True)
    return zc * lax.rsqrt(var + LN_EPS) * g + b


MERGE_TM = 512


def _merge_kernel(x_ref, mod_ref, od_ref, on_ref, gd_ref, gn_ref, wbd_ref, wbn_ref, wo_ref,
                  lng_ref, lnb_ref, o_ref):
    yd = jnp.dot(od_ref[0], wbd_ref[...], preferred_element_type=jnp.float32)
    yn = jnp.dot(on_ref[0], wbn_ref[...], preferred_element_type=jnp.float32)
    y = gd_ref[0].astype(jnp.float32) * yd + gn_ref[0].astype(jnp.float32) * yn
    mix = jnp.dot(y.astype(jnp.bfloat16), wo_ref[...], preferred_element_type=jnp.float32)
    g1 = mod_ref[0, 2:3, :]
    o_ref[0] = _layer_norm(ALPHA * x_ref[0] + g1 * mix, lng_ref[...], lnb_ref[...])


def _merge(x, mod, o_d, o_n, px, wbd, wbn, wo, ln_g, ln_b):
    B, L, D = x.shape
    tm = MERGE_TM
    gblk = D // LANES
    const = lambda shape: pl.BlockSpec(shape, lambda b, i: (0,) * len(shape))
    return pl.pallas_call(
        _merge_kernel,
        out_shape=jax.ShapeDtypeStruct((B, L, D), jnp.float32),
        grid=(B, L // tm),
        in_specs=[
            pl.BlockSpec((1, tm, D), lambda b, i: (b, i, 0)),
            pl.BlockSpec((1, N_MOD, D), lambda b, i: (b, 0, 0)),
            pl.BlockSpec((1, tm, DIFF_V_W), lambda b, i: (b, i, 0)),
            pl.BlockSpec((1, tm, NA_W), lambda b, i: (b, i, 0)),
            pl.BlockSpec((1, tm, D), lambda b, i: (b, i, GD_BLK // gblk)),
            pl.BlockSpec((1, tm, D), lambda b, i: (b, i, GN_BLK // gblk)),
            const((DIFF_V_W, D)), const((NA_W, D)), const((D, D)), const((1, D)), const((1, D)),
        ],
        out_specs=pl.BlockSpec((1, tm, D), lambda b, i: (b, i, 0)),
        compiler_params=_params("parallel", "parallel"),
        name="merge",
    )(x, mod, o_d, o_n, px, px, wbd, wbn, wo, ln_g.reshape(1, D), ln_b.reshape(1, D))


FFN_TM = 512
FFN_CHUNK = FFN_HIDDEN // 2


def _ffn_kernel(x_ref, mod_ref, wg_ref, wu_ref, wo_ref, lng_ref, lnb_ref, o_ref):
    x = x_ref[0]
    shift = mod_ref[0, 3:4, :]
    scale = mod_ref[0, 4:5, :]
    g2 = mod_ref[0, 5:6, :]
    h = (x * (1.0 + scale) + shift).astype(jnp.bfloat16)
    out = None
    for c in range(FFN_HIDDEN // FFN_CHUNK):
        sl = slice(c * FFN_CHUNK, (c + 1) * FFN_CHUNK)
        gate = jnp.dot(h, wg_ref[:, sl], preferred_element_type=jnp.float32)
        up = jnp.dot(h, wu_ref[:, sl], preferred_element_type=jnp.float32)
        act = (gate * _sigmoid(gate) * up).astype(jnp.bfloat16)
        part = jnp.dot(act, wo_ref[sl, :], preferred_element_type=jnp.float32)
        out = part if out is None else out + part
    o_ref[0] = _layer_norm(ALPHA * x + g2 * out, lng_ref[...], lnb_ref[...])


def _ffn(x, mod, wg, wu, wo, ln_g, ln_b):
    B, L, D = x.shape
    tm = FFN_TM
    const = lambda shape: pl.BlockSpec(shape, lambda b, i: (0,) * len(shape), pipeline_mode=pl.Buffered(1))
    return pl.pallas_call(
        _ffn_kernel,
        out_shape=jax.ShapeDtypeStruct((B, L, D), jnp.float32),
        grid=(B, L // tm),
        in_specs=[
            pl.BlockSpec((1, tm, D), lambda b, i: (b, i, 0)),
            pl.BlockSpec((1, N_MOD, D), lambda b, i: (b, 0, 0)),
            const((D, FFN_HIDDEN)), const((D, FFN_HIDDEN)), const((FFN_HIDDEN, D)),
            const((1, D)), const((1, D)),
        ],
        out_specs=pl.BlockSpec((1, tm, D), lambda b, i: (b, i, 0)),
        compiler_params=_params("parallel", "parallel"),
        name="ffn",
    )(x, mod, wg, wu, wo, ln_g.reshape(1, D), ln_b.reshape(1, D))


def _proj_kinds():
    kinds = []
    for width, kind in ((DIFF_QK_W, ROPE_SCALED), (GATE_W, GATE), (NA_W, SCALED), (DIFF_QK_W, ROPE),
                        (DIFF_V_W + 2 * NA_W, PLAIN)):
        kinds += [kind] * (width // PROJ_TN)
    return tuple(kinds)


def _layer(x, ctx, c, c_ctx, w_mod, b_mod, w_in, b_gate, lam_q1, lam_k1, lam_q2, lam_k2, subln_g,
           na_rpb, w_branch_diff, w_branch_na, w_out, ln1_g, ln1_b, w_ffn_in, w_ffn_out, ln2_g,
           ln2_b, layer_idx):
    B = x.shape[0]
    bf16 = jnp.bfloat16
    lam_init = 0.8 - 0.6 * math.exp(-0.3 * layer_idx)

    mod = _modulation(jnp.concatenate([c, c_ctx[None]], axis=0), w_mod, b_mod)
    mod = mod.reshape(B + 1, N_MOD, D_MODEL)
    mod_x, mod_c = mod[:B], mod[B:]

    cos, sin = _rope_tables()
    kv0 = Q_COLS + GATE_W
    w_in_bf = jnp.concatenate([_rope_column_order(w_in[:, :DIFF_QK_W]), w_in[:, Q_COLS:kv0],
                               w_in[:, DIFF_QK_W:Q_COLS], _rope_column_order(w_in[:, kv0:kv0 + DIFF_QK_W]),
                               w_in[:, kv0 + DIFF_QK_W:]], axis=1).astype(bf16)
    bias_x = jnp.zeros((1, IN_W), jnp.float32).at[0, DIFF_QK_W:DIFF_QK_W + GATE_W].set(b_gate)
    px = _projection(x, mod_x, w_in_bf, bias_x, cos, sin, _proj_kinds())
    pc = _projection(ctx, mod_c, w_in_bf[:, kv0:], jnp.zeros((1, KV_COLS), jnp.float32),
                     cos, sin, (PLAIN,) * (KV_COLS // PROJ_TN))

    o_d = _diff_attention(px, pc, lam_q1, lam_k1, lam_q2, lam_k2, subln_g, lam_init)
    o_n = _na_attention(px, pc, _na_bias_tables(na_rpb))

    x_mid = _merge(x, mod_x, o_d, o_n, px, w_branch_diff.astype(bf16), w_branch_na.astype(bf16),
                   w_out.astype(bf16), ln1_g, ln1_b)
    w_ffn_in_bf = w_ffn_in.astype(bf16)
    return _ffn(x_mid, mod_x, w_ffn_in_bf[:, :FFN_HIDDEN], w_ffn_in_bf[:, FFN_HIDDEN:],
                w_ffn_out.astype(bf16), ln2_g, ln2_b)


def kernel(x, c, ctx, c_ctx, w_mod, b_mod, w_in, b_gate, lam_q1, lam_k1, lam_q2, lam_k2, subln_g, na_rpb, w_branch_diff, w_branch_na, w_out, ln1_g, ln1_b, w_ffn_in, w_ffn_out, ln2_g, ln2_b):
    l = 0
    return _layer(x, ctx, c, c_ctx, w_mod[l], b_mod[l], w_in[l], b_gate[l], lam_q1[l], lam_k1[l],
                  lam_q2[l], lam_k2[l], subln_g[l], na_rpb[l], w_branch_diff[l], w_branch_na[l],
                  w_out[l], ln1_g[l], ln1_b[l], w_ffn_in[l], w_ffn_out[l], ln2_g[l], ln2_b[l], l)
```

```python
import functools
import math

import numpy as np
import jax
import jax.numpy as jnp
from jax import lax
from jax.experimental import pallas as pl
from jax.experimental.pallas import tpu as pltpu

D_MODEL = 1024
SEQ = 2048
CTX_LEN = 256
GRID_W = 64
ROWS = SEQ // GRID_W
DEPTH = 1

DIFF_HEADS = 8
DIFF_HD = 64
DIFF_QK_W = DIFF_HEADS * 2 * DIFF_HD
DIFF_V_W = DIFF_HEADS * 2 * DIFF_HD
NA_HEADS = 8
NA_HD = 64
NA_W = NA_HEADS * NA_HD
NA_WIN_H = 8
NA_WIN_W = 16
GATE_W = 2 * D_MODEL
Q_COLS = DIFF_QK_W + NA_W
KV_COLS = DIFF_QK_W + DIFF_V_W + 2 * NA_W
IN_W = Q_COLS + GATE_W + KV_COLS
FFN_HIDDEN = 2816
N_MOD = 6
ROPE_THETA = 10000.0
LN_EPS = 1e-5
ALPHA = (2.0 * DEPTH) ** 0.25

LANES = 128
NEG_BIG = -1e30
VMEM_LIMIT = 56 * 1024 * 1024

QD_BLK = 0
GD_BLK = DIFF_QK_W // LANES
GN_BLK = GD_BLK + D_MODEL // LANES
QN_BLK = GN_BLK + D_MODEL // LANES
KD_BLK = (Q_COLS + GATE_W) // LANES
VD_BLK = KD_BLK + DIFF_QK_W // LANES
KN_BLK = VD_BLK + DIFF_V_W // LANES
VN_BLK = KN_BLK + NA_W // LANES
C_KD_BLK = 0
C_VD_BLK = DIFF_QK_W // LANES
C_KN_BLK = C_VD_BLK + DIFF_V_W // LANES
C_VN_BLK = C_KN_BLK + NA_W // LANES

PLAIN, ROPE_SCALED, SCALED, GATE, ROPE = range(5)
PROJ_TN = 512
PROJ_TM = 512
QK_SCALE = DIFF_HD ** -0.5
LOG2E = math.log2(math.e)


def _sigmoid(x):
    return 1.0 / (1.0 + jnp.exp(-x))


def _params(*sem):
    return pltpu.CompilerParams(dimension_semantics=sem, vmem_limit_bytes=VMEM_LIMIT)


def _mod_kernel(c_ref, w_ref, b_ref, o_ref):
    c = c_ref[...]
    cs = c * _sigmoid(c)
    o_ref[...] = jnp.dot(cs, w_ref[...], preferred_element_type=jnp.float32) + b_ref[...]


def _modulation(cc, w_mod, b_mod):
    n_rows = cc.shape[0]
    n = w_mod.shape[1]
    tn = 1536
    return pl.pallas_call(
        _mod_kernel,
        out_shape=jax.ShapeDtypeStruct((n_rows, n), jnp.float32),
        grid=(n // tn,),
        in_specs=[
            pl.BlockSpec((n_rows, D_MODEL), lambda j: (0, 0)),
            pl.BlockSpec((D_MODEL, tn), lambda j: (0, j)),
            pl.BlockSpec((1, tn), lambda j: (0, j)),
        ],
        out_specs=pl.BlockSpec((n_rows, tn), lambda j: (0, j)),
        compiler_params=_params("parallel"),
        name="modulation",
    )(cc, w_mod, b_mod.reshape(1, n))


def _rope_tables():
    t = jnp.arange(SEQ)
    row = (t // GRID_W).astype(jnp.float32)
    col = (t % GRID_W).astype(jnp.float32)
    n_freq = DIFF_HD // 4
    inv = ROPE_THETA ** (-jnp.arange(n_freq, dtype=jnp.float32) / n_freq)
    ang = jnp.concatenate([row[:, None] * inv, col[:, None] * inv], axis=-1)
    reps = LANES // (2 * n_freq)
    return jnp.tile(jnp.cos(ang), (1, reps)), jnp.tile(jnp.sin(ang), (1, reps))


def _rope_column_order(w):
    d = w.shape[0]
    w = w.reshape(d, DIFF_HEADS // 2, 4, 2, 2, DIFF_HD // 4)
    return jnp.transpose(w, (0, 1, 4, 2, 3, 5)).reshape(d, DIFF_QK_W)


def _proj_kernel(kinds, x_ref, mod_ref, w_ref, bias_ref, cos_ref, sin_ref, o_ref):
    shift = mod_ref[0, 0:1, :]
    scale = mod_ref[0, 1:2, :]
    hx = (x_ref[0] * (1.0 + scale) + shift).astype(jnp.bfloat16)
    for j, kind in enumerate(kinds):
        lo = j * PROJ_TN
        acc = jnp.dot(hx, w_ref[:, lo:lo + PROJ_TN], preferred_element_type=jnp.float32)
        if kind == PLAIN:
            o_ref[0, :, lo:lo + PROJ_TN] = acc.astype(o_ref.dtype)
        elif kind == SCALED:
            o_ref[0, :, lo:lo + PROJ_TN] = (acc * (QK_SCALE * LOG2E)).astype(o_ref.dtype)
        elif kind == GATE:
            o_ref[0, :, lo:lo + PROJ_TN] = _sigmoid(acc + bias_ref[:, lo:lo + PROJ_TN]).astype(o_ref.dtype)
        else:
            out_scale = QK_SCALE * LOG2E if kind == ROPE_SCALED else 1.0
            cos = cos_ref[...] * out_scale
            sin = sin_ref[...] * out_scale
            for p in range(PROJ_TN // (2 * LANES)):
                x1 = acc[:, 2 * p * LANES:(2 * p + 1) * LANES]
                x2 = acc[:, (2 * p + 1) * LANES:(2 * p + 2) * LANES]
                o_ref[0, :, lo + 2 * p * LANES:lo + (2 * p + 1) * LANES] = (x1 * cos - x2 * sin).astype(o_ref.dtype)
                o_ref[0, :, lo + (2 * p + 1) * LANES:lo + (2 * p + 2) * LANES] = (x2 * cos + x1 * sin).astype(o_ref.dtype)


def _projection(x, mod, w, bias, cos, sin, kinds):
    B, L, D = x.shape
    n = w.shape[1]
    tm = min(PROJ_TM, L)
    assert n == PROJ_TN * len(kinds) and L % tm == 0
    per_batch = mod.shape[0] > 1
    resident = lambda shape: pl.BlockSpec(shape, lambda b, i: (0, 0), pipeline_mode=pl.Buffered(1))
    return pl.pallas_call(
        functools.partial(_proj_kernel, kinds),
        out_shape=jax.ShapeDtypeStruct((B, L, n), jnp.bfloat16),
        grid=(B, L // tm),
        in_specs=[
            pl.BlockSpec((1, tm, D), lambda b, i: (b, i, 0)),
            pl.BlockSpec((1, N_MOD, D), (lambda b, i: (b, 0, 0)) if per_batch else (lambda b, i: (0, 0, 0))),
            resident((D, n)),
            resident((1, n)),
            pl.BlockSpec((tm, LANES), lambda b, i: (i, 0)),
            pl.BlockSpec((tm, LANES), lambda b, i: (i, 0)),
        ],
        out_specs=pl.BlockSpec((1, tm, n), lambda b, i: (b, i, 0)),
        compiler_params=_params("parallel", "parallel"),
        name="projection",
    )(x, mod, w, bias, cos, sin)


DIFF_TQ = 512
DIFF_TQ_SUB = 256
DIFF_TK = 256
DIFF_SCORE_ROWS = 1024
PAIR_W = 2 * LANES
SUBLANES = 8


def _nt_dot(a, b):
    return lax.dot_general(a, b, (((1,), (1,)), ((), ())), preferred_element_type=jnp.float32)


def _diff_attn_kernel(lam_init, lq1_ref, lk1_ref, lq2_ref, lk2_ref, g_ref,
                      q_ref, kx_ref, vx_ref, kc_ref, vc_ref, o_ref, vt_ref, s_ref):
    vt_ref[:, :SEQ] = vx_ref[0].astype(jnp.float32).T.astype(jnp.bfloat16)
    vt_ref[:, SEQ:] = vc_ref[0].astype(jnp.float32).T.astype(jnp.bfloat16)

    tq = DIFF_TQ
    n_q = SEQ // tq
    n_tiles = (SEQ + CTX_LEN) // DIFF_TK
    lane = lax.broadcasted_iota(jnp.int32, (1, PAIR_W), 1)
    group = (lane & (LANES - 1)) >> 5
    first_group = 2 * (pl.program_id(1) % 2)
    lam = (jnp.exp(jnp.sum(lq1_ref[...] * lk1_ref[...], axis=-1, keepdims=True))
           - jnp.exp(jnp.sum(lq2_ref[...] * lk2_ref[...], axis=-1, keepdims=True)) + lam_init)

    score_tiles = [(kx_ref, k0, DIFF_SCORE_ROWS, k0) for k0 in range(0, SEQ, DIFF_SCORE_ROWS)]
    score_tiles.append((kc_ref, 0, CTX_LEN, SEQ))

    def masked_queries(qi, c):
        q = q_ref[0, qi * tq:(qi + 1) * tq, :]
        return jnp.where(group == first_group + c, q, jnp.zeros_like(q))

    def score_pass(qi, c):
        qp = masked_queries(qi, c)
        m8 = None
        for k_ref, k0, rows, s0 in score_tiles:
            s = _nt_dot(k_ref[0, k0:k0 + rows, :], qp)
            s_ref[qi % 2, c, s0:s0 + rows, :] = s
            mt = jnp.max(s.reshape(rows // SUBLANES, SUBLANES, tq), axis=0)
            m8 = mt if m8 is None else jnp.maximum(m8, mt)
        return jnp.max(m8, axis=0, keepdims=True)

    def value_pass(qi, c, mx):
        halves = []
        for h0 in range(0, tq, DIFF_TQ_SUB):
            m = mx[:, h0:h0 + DIFF_TQ_SUB]
            l8 = acc = None
            for t in range(n_tiles):
                e = jnp.exp2(s_ref[qi % 2, c, t * DIFF_TK:(t + 1) * DIFF_TK, h0:h0 + DIFF_TQ_SUB] - m)
                lt = jnp.sum(e.reshape(DIFF_TK // SUBLANES, SUBLANES, DIFF_TQ_SUB), axis=0)
                pv = jnp.dot(vt_ref[:, t * DIFF_TK:(t + 1) * DIFF_TK], e.astype(jnp.bfloat16),
                             preferred_element_type=jnp.float32)
                l8 = lt if l8 is None else l8 + lt
                acc = pv if acc is None else acc + pv
            halves.append(acc * (1.0 / jnp.sum(l8, axis=0, keepdims=True)))
        return jnp.concatenate(halves, axis=1)

    def finish(qi, o1, o2):
        o = o1 - lam * o2
        y = (o * lax.rsqrt(jnp.mean(o * o, axis=0, keepdims=True) + LN_EPS)).T
        o_ref[0, qi * tq:(qi + 1) * tq, :] = (y * g_ref[...] * (1.0 - lam_init)).astype(o_ref.dtype)

    prev = [score_pass(0, 0), score_pass(0, 1)]
    for qi in range(1, n_q + 1):
        cur, outs = [], []
        for c in range(2):
            if qi < n_q:
                cur.append(score_pass(qi, c))
            outs.append(value_pass(qi - 1, c, prev[c]))
        finish(qi - 1, *outs)
        prev = cur


def _diff_attention(px, pc, lam_q1, lam_k1, lam_q2, lam_k2, subln_g, lam_init):
    B = px.shape[0]
    vec = lambda a: a.reshape(1, -1)
    small = lambda n: pl.BlockSpec((1, n), lambda b, h: (0, 0))
    qd_pair = QD_BLK * LANES // PAIR_W
    kd_pair = KD_BLK * LANES // PAIR_W
    c_kd_pair = C_KD_BLK * LANES // PAIR_W
    return pl.pallas_call(
        functools.partial(_diff_attn_kernel, lam_init),
        out_shape=jax.ShapeDtypeStruct((B, SEQ, DIFF_V_W), jnp.bfloat16),
        grid=(B, DIFF_HEADS),
        in_specs=[
            small(DIFF_HD), small(DIFF_HD), small(DIFF_HD), small(DIFF_HD), small(2 * DIFF_HD),
            pl.BlockSpec((1, SEQ, PAIR_W), lambda b, h: (b, 0, qd_pair + h // 2)),
            pl.BlockSpec((1, SEQ, PAIR_W), lambda b, h: (b, 0, kd_pair + h // 2)),
            pl.BlockSpec((1, SEQ, LANES), lambda b, h: (b, 0, VD_BLK + h)),
            pl.BlockSpec((1, CTX_LEN, PAIR_W), lambda b, h: (b, 0, c_kd_pair + h // 2)),
            pl.BlockSpec((1, CTX_LEN, LANES), lambda b, h: (b, 0, C_VD_BLK + h)),
        ],
        out_specs=pl.BlockSpec((1, SEQ, LANES), lambda b, h: (b, 0, h)),
        scratch_shapes=[pltpu.VMEM((LANES, SEQ + CTX_LEN), jnp.bfloat16),
                        pltpu.VMEM((2, 2, SEQ + CTX_LEN, DIFF_TQ), jnp.float32)],
        compiler_params=_params("parallel", "parallel"),
        name="diff_attention",
    )(vec(lam_q1), vec(lam_k1), vec(lam_q2), vec(lam_k2), vec(subln_g), px, px, px, pc, pc)


NA_QROWS = 4
NA_KROWS = NA_QROWS + NA_WIN_H
NA_TQ = NA_QROWS * GRID_W
NA_TK = NA_KROWS * GRID_W
NA_BLOCKS = ROWS // NA_QROWS
NA_PATTERNS = 3


def _na_key_row_start(i):
    return min(max(i * NA_QROWS - NA_WIN_H // 2, 0), ROWS - NA_KROWS)


def _na_pattern(i):
    return 0 if i == 0 else (2 if i == NA_BLOCKS - 1 else 1)


def _na_bias_tables(rpb):
    wh = min(NA_WIN_H, ROWS)
    n_dr = 2 * NA_WIN_H - 1
    col = np.arange(GRID_W)
    col_start = np.clip(col - NA_WIN_W // 2, 0, GRID_W - NA_WIN_W)
    col_ok = (col[None, :] >= col_start[:, None]) & (col[None, :] < col_start[:, None] + NA_WIN_W)
    col_idx = np.clip(col[None, :] - col[:, None] + NA_WIN_W - 1, 0, 2 * NA_WIN_W - 2)
    r = jnp.where(col_ok.T[None, None], rpb[:, :, col_idx.T] * LOG2E, NEG_BIG)
    r = jnp.concatenate([r, jnp.full((NA_HEADS, 1, GRID_W, GRID_W), NEG_BIG, r.dtype)], axis=1)
    assert all(_na_key_row_start(j) == (j - 1) * NA_QROWS for j in range(1, NA_BLOCKS - 1))
    idx = np.full((NA_PATTERNS, NA_KROWS, NA_QROWS), n_dr, np.int32)
    for i in (0, 1, NA_BLOCKS - 1):
        for a in range(NA_QROWS):
            qr = i * NA_QROWS + a
            rs = min(max(qr - wh // 2, 0), ROWS - wh)
            for c in range(NA_KROWS):
                kr = _na_key_row_start(i) + c
                if rs <= kr < rs + wh:
                    idx[_na_pattern(i), c, a] = kr - qr + NA_WIN_H - 1
    t = r[:, idx]
    t = jnp.transpose(t, (0, 1, 2, 4, 3, 5))
    return t.reshape(NA_HEADS, NA_PATTERNS, NA_TK, NA_TQ).astype(jnp.float32)


def _na_kernel(bias_ref, q_ref, k_ref, v_ref, kc_ref, vc_ref, o_ref, vt_ref):
    vt_ref[:, :SEQ] = v_ref[0].astype(jnp.float32).T.astype(jnp.bfloat16)
    vt_ref[:, SEQ:] = vc_ref[0].astype(jnp.float32).T.astype(jnp.bfloat16)
    lane = lax.broadcasted_iota(jnp.int32, (1, LANES), 1)
    kc = kc_ref[0]

    def group_max(s):
        return jnp.max(s.reshape(s.shape[0] // SUBLANES, SUBLANES, s.shape[1]), axis=0)

    def group_sum(e):
        return jnp.sum(e.reshape(e.shape[0] // SUBLANES, SUBLANES, e.shape[1]), axis=0)

    def scores(i, hh):
        k0 = _na_key_row_start(i) * GRID_W
        q = q_ref[0, i * NA_TQ:(i + 1) * NA_TQ, :]
        qh = jnp.where((lane >= hh * NA_HD) & (lane < (hh + 1) * NA_HD), q, jnp.zeros_like(q))
        s_loc = _nt_dot(k_ref[0, k0:k0 + NA_TK, :], qh) + bias_ref[hh, _na_pattern(i)]
        s_ctx = _nt_dot(kc, qh)
        return s_loc, s_ctx

    def attend(i, hh, s_loc, s_ctx):
        k0 = _na_key_row_start(i) * GRID_W
        m = jnp.max(jnp.maximum(group_max(s_loc), group_max(s_ctx)), axis=0, keepdims=True)
        e_loc = jnp.exp2(s_loc - m)
        e_ctx = jnp.exp2(s_ctx - m)
        l = jnp.sum(group_sum(e_loc) + group_sum(e_ctx), axis=0, keepdims=True)
        pv = (jnp.dot(vt_ref[:, k0:k0 + NA_TK], e_loc.astype(jnp.bfloat16),
                      preferred_element_type=jnp.float32)
              + jnp.dot(vt_ref[:, SEQ:], e_ctx.astype(jnp.bfloat16), preferred_element_type=jnp.float32))
        return pv[hh * NA_HD:(hh + 1) * NA_HD] * (1.0 / l)

    items = [(i, hh) for i in range(NA_BLOCKS) for hh in range(2)]
    pending = scores(*items[0])
    heads = []
    for n, (i, hh) in enumerate(items):
        nxt = scores(*items[n + 1]) if n + 1 < len(items) else None
        heads.append(attend(i, hh, *pending))
        pending = nxt
        if hh == 1:
            o_ref[0, i * NA_TQ:(i + 1) * NA_TQ, :] = jnp.concatenate(heads, axis=0).T.astype(o_ref.dtype)
            heads = []


def _na_attention(px, pc, bias):
    B = px.shape[0]
    return pl.pallas_call(
        _na_kernel,
        out_shape=jax.ShapeDtypeStruct((B, SEQ, NA_W), jnp.bfloat16),
        grid=(NA_HEADS // 2, B),
        in_specs=[
            pl.BlockSpec((2, NA_PATTERNS, NA_TK, NA_TQ), lambda hp, b: (hp, 0, 0, 0)),
            pl.BlockSpec((1, SEQ, LANES), lambda hp, b: (b, 0, QN_BLK + hp)),
            pl.BlockSpec((1, SEQ, LANES), lambda hp, b: (b, 0, KN_BLK + hp)),
            pl.BlockSpec((1, SEQ, LANES), lambda hp, b: (b, 0, VN_BLK + hp)),
            pl.BlockSpec((1, CTX_LEN, LANES), lambda hp, b: (b, 0, C_KN_BLK + hp)),
            pl.BlockSpec((1, CTX_LEN, LANES), lambda hp, b: (b, 0, C_VN_BLK + hp)),
        ],
        out_specs=pl.BlockSpec((1, SEQ, LANES), lambda hp, b: (b, 0, hp)),
        scratch_shapes=[pltpu.VMEM((LANES, SEQ + CTX_LEN), jnp.bfloat16)],
        compiler_params=_params("parallel", "parallel"),
        name="na_attention",
    )(bias, px, px, px, pc, pc)


def _layer_norm(z, g, b):
    mu = jnp.mean(z, axis=-1, keepdims=True)
    zc = z - mu
    var = jnp.mean(zc * zc, axis=-1, keepdims=True)
    return zc * lax.rsqrt(var + LN_EPS) * g + b


MERGE_TM = 1024
MERGE_SUB = 256


def _merge_kernel(x_ref, mod_ref, od_ref, on_ref, gd_ref, gn_ref, wbd_ref, wbn_ref, wo_ref,
                  lng_ref, lnb_ref, o_ref):
    g1 = mod_ref[0, 2:3, :]
    n_sub = x_ref.shape[1] // MERGE_SUB

    def branches(r):
        rows = slice(r * MERGE_SUB, (r + 1) * MERGE_SUB)
        return (jnp.dot(od_ref[0, rows, :], wbd_ref[...], preferred_element_type=jnp.float32),
                jnp.dot(on_ref[0, rows, :], wbn_ref[...], preferred_element_type=jnp.float32))

    pending = branches(0)
    for r in range(n_sub):
        rows = slice(r * MERGE_SUB, (r + 1) * MERGE_SUB)
        yd, yn = pending
        if r + 1 < n_sub:
            pending = branches(r + 1)
        y = gd_ref[0, rows, :].astype(jnp.float32) * yd + gn_ref[0, rows, :].astype(jnp.float32) * yn
        mix = jnp.dot(y.astype(jnp.bfloat16), wo_ref[...], preferred_element_type=jnp.float32)
        o_ref[0, rows, :] = _layer_norm(ALPHA * x_ref[0, rows, :] + g1 * mix, lng_ref[...], lnb_ref[...])


def _merge(x, mod, o_d, o_n, px, wbd, wbn, wo, ln_g, ln_b):
    B, L, D = x.shape
    tm = MERGE_TM
    gblk = D // LANES
    const = lambda shape: pl.BlockSpec(shape, lambda b, i: (0,) * len(shape), pipeline_mode=pl.Buffered(1))
    return pl.pallas_call(
        _merge_kernel,
        out_shape=jax.ShapeDtypeStruct((B, L, D), jnp.float32),
        grid=(B, L // tm),
        in_specs=[
            pl.BlockSpec((1, tm, D), lambda b, i: (b, i, 0)),
            pl.BlockSpec((1, N_MOD, D), lambda b, i: (b, 0, 0)),
            pl.BlockSpec((1, tm, DIFF_V_W), lambda b, i: (b, i, 0)),
            pl.BlockSpec((1, tm, NA_W), lambda b, i: (b, i, 0)),
            pl.BlockSpec((1, tm, D), lambda b, i: (b, i, GD_BLK // gblk)),
            pl.BlockSpec((1, tm, D), lambda b, i: (b, i, GN_BLK // gblk)),
            const((DIFF_V_W, D)), const((NA_W, D)), const((D, D)), const((1, D)), const((1, D)),
        ],
        out_specs=pl.BlockSpec((1, tm, D), lambda b, i: (b, i, 0)),
        compiler_params=_params("parallel", "parallel"),
        name="merge",
    )(x, mod, o_d, o_n, px, px, wbd, wbn, wo, ln_g.reshape(1, D), ln_b.reshape(1, D))


FFN_TM = 512
FFN_CHUNKS = ((0, 1024), (1024, 1024), (2048, 768))
assert sum(w for _, w in FFN_CHUNKS) == FFN_HIDDEN


def _ffn_kernel(x_ref, mod_ref, wg_ref, wu_ref, wo_ref, lng_ref, lnb_ref, o_ref):
    x = x_ref[0]
    shift = mod_ref[0, 3:4, :]
    scale = mod_ref[0, 4:5, :]
    g2 = mod_ref[0, 5:6, :]
    h = (x * (1.0 + scale) + shift).astype(jnp.bfloat16)

    def gate_up(c):
        lo, width = FFN_CHUNKS[c]
        return (jnp.dot(h, wg_ref[:, lo:lo + width], preferred_element_type=jnp.float32),
                jnp.dot(h, wu_ref[:, lo:lo + width], preferred_element_type=jnp.float32))

    out = None
    pending = gate_up(0)
    for c, (lo, width) in enumerate(FFN_CHUNKS):
        gate, up = pending
        if c + 1 < len(FFN_CHUNKS):
            pending = gate_up(c + 1)
        act = (gate * _sigmoid(gate) * up).astype(jnp.bfloat16)
        part = jnp.dot(act, wo_ref[lo:lo + width, :], preferred_element_type=jnp.float32)
        out = part if out is None else out + part
    o_ref[0] = _layer_norm(ALPHA * x + g2 * out, lng_ref[...], lnb_ref[...])


def _ffn(x, mod, wg, wu, wo, ln_g, ln_b):
    B, L, D = x.shape
    tm = FFN_TM
    const = lambda shape: pl.BlockSpec(shape, lambda b, i: (0,) * len(shape), pipeline_mode=pl.Buffered(1))
    return pl.pallas_call(
        _ffn_kernel,
        out_shape=jax.ShapeDtypeStruct((B, L, D), jnp.float32),
        grid=(B, L // tm),
        in_specs=[
            pl.BlockSpec((1, tm, D), lambda b, i: (b, i, 0)),
            pl.BlockSpec((1, N_MOD, D), lambda b, i: (b, 0, 0)),
            const((D, FFN_HIDDEN)), const((D, FFN_HIDDEN)), const((FFN_HIDDEN, D)),
            const((1, D)), const((1, D)),
        ],
        out_specs=pl.BlockSpec((1, tm, D), lambda b, i: (b, i, 0)),
        compiler_params=_params("parallel", "parallel"),
        name="ffn",
    )(x, mod, wg, wu, wo, ln_g.reshape(1, D), ln_b.reshape(1, D))


def _proj_kinds():
    kinds = []
    for width, kind in ((DIFF_QK_W, ROPE_SCALED), (GATE_W, GATE), (NA_W, SCALED), (DIFF_QK_W, ROPE),
                        (DIFF_V_W + 2 * NA_W, PLAIN)):
        kinds += [kind] * (width // PROJ_TN)
    return tuple(kinds)


def _layer(x, ctx, c, c_ctx, w_mod, b_mod, w_in, b_gate, lam_q1, lam_k1, lam_q2, lam_k2, subln_g,
           na_rpb, w_branch_diff, w_branch_na, w_out, ln1_g, ln1_b, w_ffn_in, w_ffn_out, ln2_g,
           ln2_b, layer_idx):
    B = x.shape[0]
    bf16 = jnp.bfloat16
    lam_init = 0.8 - 0.6 * math.exp(-0.3 * layer_idx)

    mod = _modulation(jnp.concatenate([c, c_ctx[None]], axis=0), w_mod, b_mod)
    mod = mod.reshape(B + 1, N_MOD, D_MODEL)
    mod_x, mod_c = mod[:B], mod[B:]

    cos, sin = _rope_tables()
    kv0 = Q_COLS + GATE_W
    w_in_bf = jnp.concatenate([_rope_column_order(w_in[:, :DIFF_QK_W]), w_in[:, Q_COLS:kv0],
                               w_in[:, DIFF_QK_W:Q_COLS], _rope_column_order(w_in[:, kv0:kv0 + DIFF_QK_W]),
                               w_in[:, kv0 + DIFF_QK_W:]], axis=1).astype(bf16)
    bias_x = jnp.zeros((1, IN_W), jnp.float32).at[0, DIFF_QK_W:DIFF_QK_W + GATE_W].set(b_gate)
    px = _projection(x, mod_x, w_in_bf, bias_x, cos, sin, _proj_kinds())
    pc = _projection(ctx, mod_c, w_in_bf[:, kv0:], jnp.zeros((1, KV_COLS), jnp.float32),
                     cos, sin, (PLAIN,) * (KV_COLS // PROJ_TN))

    o_d = _diff_attention(px, pc, lam_q1, lam_k1, lam_q2, lam_k2, subln_g, lam_init)
    o_n = _na_attention(px, pc, _na_bias_tables(na_rpb))

    x_mid = _merge(x, mod_x, o_d, o_n, px, w_branch_diff.astype(bf16), w_branch_na.astype(bf16),
                   w_out.astype(bf16), ln1_g, ln1_b)
    w_ffn_in_bf = w_ffn_in.astype(bf16)
    return _ffn(x_mid, mod_x, w_ffn_in_bf[:, :FFN_HIDDEN], w_ffn_in_bf[:, FFN_HIDDEN:],
                w_ffn_out.astype(bf16), ln2_g, ln2_b)


def kernel(x, c, ctx, c_ctx, w_mod, b_mod, w_in, b_gate, lam_q1, lam_k1, lam_q2, lam_k2, subln_g, na_rpb, w_branch_diff, w_branch_na, w_out, ln1_g, ln1_b, w_ffn_in, w_ffn_out, ln2_g, ln2_b):
    l = 0
    return _layer(x, ctx, c, c_ctx, w_mod[l], b_mod[l], w_in[l], b_gate[l], lam_q1[l], lam_k1[l],
                  lam_q2[l], lam_k2[l], subln_g[l], na_rpb[l], w_branch_diff[l], w_branch_na[l],
                  w_out[l], ln1_g[l], ln1_b[l], w_ffn_in[l], w_ffn_out[l], ln2_g[l], ln2_b[l], l)
```
